```python
import math
import jax, jax.numpy as jnp
from jax import lax
import numpy as np

D_MODEL = 1024
BATCH = 8
SEQ = 2048
DEPTH = 4
DEC_BATCH = 128
DEC_SEQ = 4
PAST_LEN = 16384
PAGE_SIZE = 128

D_CONV = 256
CONV_W = 3
H_R = 6
HD_R = 64
D_RWKV = H_R * HD_R
LORA_W = 32
LORA_A = 32
LORA_G = 64
P_RWKV = 3 * D_RWKV + LORA_W + LORA_A + LORA_G
RWKV_SPLITS = (D_RWKV, 2 * D_RWKV, 3 * D_RWKV, 3 * D_RWKV + LORA_W, 3 * D_RWKV + LORA_W + LORA_A)
H_M = 4
HD_M = 96
D_MLSTM = H_M * HD_M
MCONV_W = 4
CHUNK = 64
P_CONV = 3 * D_CONV
P_MLSTM = 2 * D_MLSTM
P_IN = P_CONV + P_RWKV + P_MLSTM
D_MIX = D_CONV + D_RWKV + D_MLSTM
D_FF = 2816
RMS_EPS = 1e-6
GN_EPS_R = 64e-5
GN_EPS_M = 1e-6

kernel_name = "hymba_conv_rwkv7_mlstm_macaron_step"


def rmsnorm(x, g):
    xf = x.astype(jnp.float32)
    y = xf * lax.rsqrt(jnp.mean(xf * xf, axis=-1, keepdims=True) + RMS_EPS)
    return (y * g.astype(jnp.float32)).astype(x.dtype)


def head_norm(x, eps):
    xf = x.astype(jnp.float32)
    xc = xf - jnp.mean(xf, axis=-1, keepdims=True)
    y = xc * lax.rsqrt(jnp.mean(xc * xc, axis=-1, keepdims=True) + eps)
    return y.reshape(*x.shape[:-2], x.shape[-2] * x.shape[-1])


def swiglu(x, wg, wu, wd):
    return (jax.nn.silu(x @ wg) * (x @ wu)) @ wd


def causal_dwconv(buf, x, w):
    T = x.shape[1]
    xp = jnp.concatenate([buf, x], axis=1)
    y = sum(w[j] * xp[:, j:j + T] for j in range(w.shape[0]))
    return y, xp[:, T:]


def short_conv_mixer(pc, buf, conv_w):
    b_gate, c_gate, h = jnp.split(pc, 3, axis=-1)
    z, new_buf = causal_dwconv(buf, c_gate * h, conv_w)
    return b_gate * z, new_buf


def wkv_scan(S0, r, w, k, v, kk, a):
    f32 = jnp.float32

    def step(S, inp):
        r_t, w_t, k_t, v_t, kk_t, a_t = inp
        sa = jnp.einsum('bhvk,bhk->bhv', S, -kk_t)
        S = S * w_t[:, :, None, :] + sa[..., None] * (kk_t * a_t)[:, :, None, :] + v_t[..., None] * k_t[:, :, None, :]
        return S, jnp.einsum('bhvk,bhk->bhv', S, r_t)

    xs = tuple(jnp.moveaxis(t.astype(f32), 1, 0) for t in (r, w, k, v, kk, a))
    S, ys = lax.scan(step, S0.astype(f32), xs)
    return S.astype(S0.dtype), jnp.moveaxis(ys, 0, 1)


def rwkv7_mixer(p, shift_buf, S0, mu, w0, w_up, a0, a_up, g_up, k_k, k_a, r_k, ln_w, ln_b):
    Bsz, T, _ = p.shape
    prev = jnp.concatenate([shift_buf, p[:, :-1]], axis=1)
    ps = p + (prev - p) * mu
    r, k, v, xw, xa, xg = jnp.split(ps, RWKV_SPLITS, axis=-1)
    w_log = -jax.nn.softplus(-(w0 + jnp.tanh(xw) @ w_up)) - 0.5
    decay = jnp.exp(-jnp.exp(w_log.astype(jnp.float32)))
    a = jax.nn.sigmoid(a0 + xa @ a_up)
    g = jax.nn.sigmoid(xg) @ g_up
    hd = lambda t: t.reshape(Bsz, T, H_R, HD_R)
    kk = hd(k * k_k).astype(jnp.float32)
    kk = kk * lax.rsqrt(jnp.maximum(jnp.sum(kk * kk, axis=-1, keepdims=True), 1e-24))
    k = k * (1.0 + (a - 1.0) * k_a)
    rh, kh, vh = hd(r), hd(k), hd(v)
    S_T, y = wkv_scan(S0, rh, hd(decay), kh, vh, kk, hd(a))
    y = head_norm(y, GN_EPS_R) * ln_w + ln_b
    bonus = (jnp.sum(rh * kh * r_k, axis=-1, keepdims=True) * vh).reshape(Bsz, T, D_RWKV)
    return (y + bonus).astype(p.dtype) * g, p[:, -1:], S_T


def mlstm_chunkwise(q, k, v, ig, lf, C0, n0, m0):
    B, H, T, d = q.shape
    L = math.gcd(T, CHUNK)
    N = T // L
    to_chunks = lambda t: jnp.moveaxis(t.reshape(B, H, N, L, *t.shape[3:]), 2, 0)
    causal = jnp.tril(jnp.ones((L, L), dtype=bool))

    def step(carry, inp):
        C, n, m = carry
        qc, kc, vc, ic, fc = inp
        b = jnp.cumsum(fc, axis=-1)
        Dm = jnp.where(causal, b[..., :, None] - b[..., None, :] + ic[..., None, :], -jnp.inf)
        inter = b + m[..., None]
        m_t = jnp.maximum(inter, jnp.max(Dm, axis=-1))
        s = jnp.einsum('bhtd,bhsd->bhts', qc, kc) * jnp.exp(Dm - m_t[..., None])
        wI = jnp.exp(inter - m_t)
        num = jnp.einsum('bhts,bhsd->bhtd', s, vc) + wI[..., None] * jnp.einsum('bhvk,bhtk->bhtv', C, qc)
        den = jnp.sum(s, axis=-1) + wI * jnp.einsum('bhk,bhtk->bht', n, qc)
        h = num / jnp.maximum(jnp.abs(den), jnp.exp(-m_t))[..., None]
        bL = b[..., -1]
        m_new = m_t[..., -1]
        wS = jnp.exp(bL[..., None] - b + ic - m_new[..., None])
        wP = jnp.exp(bL + m - m_new)
        C = wP[..., None, None] * C + jnp.einsum('bhs,bhsv,bhsk->bhvk', wS, vc, kc)
        n = wP[..., None] * n + jnp.einsum('bhs,bhsk->bhk', wS, kc)
        return (C, n, m_new), h

    (C, n, m), hs = lax.scan(step, (C0, n0, m0), tuple(to_chunks(t) for t in (q, k, v, ig, lf)))
    h = jnp.moveaxis(hs, 0, 2).reshape(B, H, T, d)
    return h, C, n, m


def mlstm_mixer(pm, conv_buf, C0, n0, m0, conv_w, conv_b, wq, wk, wv, w_gate, b_gate, gn_w, skip):
    f32 = jnp.float32
    Bsz, T, _ = pm.shape
    u, o = jnp.split(pm, 2, axis=-1)
    uc, new_buf = causal_dwconv(conv_buf, u, conv_w)
    uc = jax.nn.silu(uc + conv_b)
    uch = uc.reshape(Bsz, T, H_M, HD_M)
    q = jnp.einsum('bthd,hde->bthe', uch, wq)
    k = jnp.einsum('bthd,hde->bthe', uch, wk)
    v = jnp.einsum('bthd,hde->bthe', u.reshape(Bsz, T, H_M, HD_M), wv)
    gp = (jnp.concatenate([q, k, v], axis=-1).reshape(Bsz, T, 3 * D_MLSTM) @ w_gate + b_gate).astype(f32)
    ig = jnp.moveaxis(gp[..., :H_M], -1, 1)
    lf = jnp.moveaxis(jax.nn.log_sigmoid(gp[..., H_M:]), -1, 1)
    tr = lambda t: jnp.moveaxis(t, 2, 1).astype(f32)
    h, C, n, m = mlstm_chunkwise(tr(q), tr(k) * (HD_M ** -0.5), tr(v), ig, lf,
                                 C0.astype(f32), n0.astype(f32), m0.astype(f32))
    y = head_norm(jnp.moveaxis(h, 1, 2), GN_EPS_M) * gn_w + skip * uc
    y = y.astype(pm.dtype) * jax.nn.sigmoid(o)
    return y, new_buf, C.astype(C0.dtype), n.astype(n0.dtype), m.astype(m0.dtype)


def run_trunk(x, conv_st, shift_st, wkv_st, mconv_st, mC_st, mn_st, mm_st, W):
    outs = [[], [], [], [], [], [], []]
    for l in range(DEPTH):
        x = x + 0.5 * swiglu(rmsnorm(x, W['norm_ff'][l, 0]), W['ffn_gate'][l, 0], W['ffn_up'][l, 0], W['ffn_down'][l, 0])
        hN = rmsnorm(x, W['norm_mix'][l])
        pc, pr, pm = jnp.split(hN @ W['w_in'][l], [P_CONV, P_CONV + P_RWKV], axis=-1)
        yc, c_new = short_conv_mixer(pc, conv_st[l], W['conv_w'][l])
        yr, sh_new, S_new = rwkv7_mixer(pr, shift_st[l], wkv_st[l], W['rwkv_mu'][l], W['rwkv_w0'][l],
                                        W['rwkv_w_up'][l], W['rwkv_a0'][l], W['rwkv_a_up'][l], W['rwkv_g_up'][l],
                                        W['rwkv_k_k'][l], W['rwkv_k_a'][l], W['rwkv_r_k'][l],
                                        W['rwkv_ln_w'][l], W['rwkv_ln_b'][l])
        ym, mc_new, C_new, n_new, m_new = mlstm_mixer(pm, mconv_st[l], mC_st[l], mn_st[l], mm_st[l],
                                                      W['mlstm_conv_w'][l], W['mlstm_conv_b'][l], W['mlstm_wq'][l],
                                                      W['mlstm_wk'][l], W['mlstm_wv'][l], W['mlstm_w_gate'][l],
                                                      W['mlstm_b_gate'][l], W['mlstm_gn_w'][l], W['mlstm_skip'][l])
        x = x + jnp.concatenate([yc, yr, ym], axis=-1) @ W['w_out'][l]
        x = x + 0.5 * swiglu(rmsnorm(x, W['norm_ff'][l, 1]), W['ffn_gate'][l, 1], W['ffn_up'][l, 1], W['ffn_down'][l, 1])
        for lst, s in zip(outs, (c_new, sh_new, S_new, mc_new, C_new, n_new, m_new)):
            lst.append(s)
    return rmsnorm(x, W['norm_final']), [jnp.stack(o) for o in outs]


def setup_inputs(seed: int = 0) -> dict:
    key = jax.random.key(seed)
    ks = iter(jax.random.split(key, 48))
    f32 = jnp.float32
    nrm = lambda shape, s: jax.random.normal(next(ks), shape, f32) * s
    uni = lambda shape, lo, hi: jax.random.uniform(next(ks), shape, f32, minval=lo, maxval=hi)
    return {
        "x_prompt": nrm((BATCH, SEQ, D_MODEL), 1.0),
        "x_sample": nrm((DEC_BATCH, DEC_SEQ, D_MODEL), 1.0),
        "cache_conv": nrm((DEPTH, DEC_BATCH, CONV_W - 1, D_CONV), 1.0),
        "cache_shift": nrm((DEPTH, DEC_BATCH, 1, P_RWKV), 1.0),
        "state_wkv": nrm((DEPTH, DEC_BATCH, H_R, HD_R, HD_R), 0.5),
        "cache_mconv": nrm((DEPTH, DEC_BATCH, MCONV_W - 1, D_MLSTM), 1.0),
        "state_mC": nrm((DEPTH, DEC_BATCH, H_M, HD_M, HD_M), 0.3),
        "state_mn": nrm((DEPTH, DEC_BATCH, H_M, HD_M), 0.3),
        "state_mm": nrm((DEPTH, DEC_BATCH, H_M), 1.0),
        "norm_ff": 1.0 + nrm((DEPTH, 2, D_MODEL), 0.02),
        "ffn_gate": nrm((DEPTH, 2, D_MODEL, D_FF), D_MODEL ** -0.5),
        "ffn_up": nrm((DEPTH, 2, D_MODEL, D_FF), D_MODEL ** -0.5),
        "ffn_down": nrm((DEPTH, 2, D_FF, D_MODEL), D_FF ** -0.5),
        "norm_mix": 1.0 + nrm((DEPTH, D_MODEL), 0.02),
        "w_in": nrm((DEPTH, D_MODEL, P_IN), D_MODEL ** -0.5),
        "w_out": nrm((DEPTH, D_MIX, D_MODEL), D_MIX ** -0.5),
        "conv_w": nrm((DEPTH, CONV_W, D_CONV), CONV_W ** -0.5),
        "rwkv_mu": uni((DEPTH, P_RWKV), 0.0, 1.0),
        "rwkv_w0": uni((DEPTH, D_RWKV), -6.5, -1.5),
        "rwkv_w_up": nrm((DEPTH, LORA_W, D_RWKV), 0.1),
        "rwkv_a0": nrm((DEPTH, D_RWKV), 0.1),
        "rwkv_a_up": nrm((DEPTH, LORA_A, D_RWKV), 0.1),
        "rwkv_g_up": nrm((DEPTH, LORA_G, D_RWKV), LORA_G ** -0.5),
        "rwkv_k_k": 0.85 + nrm((DEPTH, D_RWKV), 0.02),
        "rwkv_k_a": 1.0 + nrm((DEPTH, D_RWKV), 0.02),
        "rwkv_r_k": nrm((DEPTH, H_R, HD_R), 0.1),
        "rwkv_ln_w": 1.0 + nrm((DEPTH, D_RWKV), 0.02),
        "rwkv_ln_b": nrm((DEPTH, D_RWKV), 0.01),
        "mlstm_conv_w": nrm((DEPTH, MCONV_W, D_MLSTM), MCONV_W ** -0.5),
        "mlstm_conv_b": nrm((DEPTH, D_MLSTM), 0.01),
        "mlstm_wq": nrm((DEPTH, H_M, HD_M, HD_M), HD_M ** -0.5),
        "mlstm_wk": nrm((DEPTH, H_M, HD_M, HD_M), HD_M ** -0.5),
        "mlstm_wv": nrm((DEPTH, H_M, HD_M, HD_M), HD_M ** -0.5),
        "mlstm_w_gate": nrm((DEPTH, 3 * D_MLSTM, 2 * H_M), 0.02),
        "mlstm_b_gate": jnp.concatenate([nrm((DEPTH, H_M), 0.1), uni((DEPTH, H_M), 3.0, 6.0)], axis=-1),
        "mlstm_gn_w": 1.0 + nrm((DEPTH, D_MLSTM), 0.02),
        "mlstm_skip": 1.0 + nrm((DEPTH, D_MLSTM), 0.02),
        "norm_final": 1.0 + nrm((D_MODEL,), 0.02),
    }


def reference(x_prompt, x_sample, cache_conv, cache_shift, state_wkv, cache_mconv, state_mC, state_mn, state_mm,
              norm_ff, ffn_gate, ffn_up, ffn_down, norm_mix, w_in, w_out, conv_w,
              rwkv_mu, rwkv_w0, rwkv_w_up, rwkv_a0, rwkv_a_up, rwkv_g_up, rwkv_k_k, rwkv_k_a, rwkv_r_k,
              rwkv_ln_w, rwkv_ln_b,
              mlstm_conv_w, mlstm_conv_b, mlstm_wq, mlstm_wk, mlstm_wv, mlstm_w_gate, mlstm_b_gate,
              mlstm_gn_w, mlstm_skip, norm_final):
    W = dict(norm_ff=norm_ff, ffn_gate=ffn_gate, ffn_up=ffn_up, ffn_down=ffn_down, norm_mix=norm_mix,
             w_in=w_in, w_out=w_out, conv_w=conv_w, rwkv_mu=rwkv_mu, rwkv_w0=rwkv_w0, rwkv_w_up=rwkv_w_up,
             rwkv_a0=rwkv_a0, rwkv_a_up=rwkv_a_up, rwkv_g_up=rwkv_g_up, rwkv_k_k=rwkv_k_k, rwkv_k_a=rwkv_k_a,
             rwkv_r_k=rwkv_r_k, rwkv_ln_w=rwkv_ln_w, rwkv_ln_b=rwkv_ln_b, mlstm_conv_w=mlstm_conv_w,
             mlstm_conv_b=mlstm_conv_b, mlstm_wq=mlstm_wq, mlstm_wk=mlstm_wk, mlstm_wv=mlstm_wv,
             mlstm_w_gate=mlstm_w_gate, mlstm_b_gate=mlstm_b_gate, mlstm_gn_w=mlstm_gn_w,
             mlstm_skip=mlstm_skip, norm_final=norm_final)
    Bp = x_prompt.shape[0]
    dt = x_prompt.dtype
    y_prompt, p_states = run_trunk(
        x_prompt,
        jnp.zeros((DEPTH, Bp, CONV_W - 1, D_CONV), dt),
        jnp.zeros((DEPTH, Bp, 1, P_RWKV), dt),
        jnp.zeros((DEPTH, Bp, H_R, HD_R, HD_R), dt),
        jnp.zeros((DEPTH, Bp, MCONV_W - 1, D_MLSTM), dt),
        jnp.zeros((DEPTH, Bp, H_M, HD_M, HD_M), dt),
        jnp.zeros((DEPTH, Bp, H_M, HD_M), dt),
        jnp.zeros((DEPTH, Bp, H_M), dt),
        W)
    y_sample, s_states = run_trunk(x_sample, cache_conv, cache_shift, state_wkv, cache_mconv,
                                   state_mC, state_mn, state_mm, W)
    p_conv, p_shift, p_wkv, p_mconv, p_mC, p_mn, p_mm = p_states
    s_conv, s_shift, s_wkv, s_mconv, s_mC, s_mn, s_mm = s_states
    return (y_prompt, y_sample, p_conv, p_shift, p_wkv, p_mconv, p_mC, p_mn, p_mm,
            s_conv, s_shift, s_wkv, s_mconv, s_mC, s_mn, s_mm)
```

```python
import functools

import jax
import jax.numpy as jnp
from jax import lax
from jax.experimental import pallas as pl
from jax.experimental.pallas import tpu as pltpu

F32 = jnp.float32
BF16 = jnp.bfloat16
HI = lax.Precision.HIGHEST

LANES = 128
H_R, HD_R = 6, 64
D_RWKV = H_R * HD_R
N_PAIR = H_R // 2
LORA_W, LORA_A, LORA_G = 32, 32, 64
P_RWKV = 3 * D_RWKV + LORA_W + LORA_A + LORA_G
H_M, HD_M = 4, 96
D_MLSTM = H_M * HD_M
D_MPAD = H_M * LANES
D_CONV = 256
CONV_W, MCONV_W = 3, 4
RMS_EPS = 1e-6
GN_EPS_R = 64e-5
GN_EPS_M = 1e-6
SAMPLE_PAD_T = 8
MIX_CHUNK = 64
CONV_CHUNK = 512
TOKEN_TILE = 512
FFN_COLS = 256
VMEM_LIMIT = 48 * 1024 * 1024

NN = ((1,), (0,))
NT = ((1,), (1,))
TN = ((0,), (0,))

MODE_PROJ = "b1"
MODE_SCAN = "b3"


def _mm(a, b, dims=NN, mode="b1"):
    dn = (dims, ((), ()))
    if mode == "hi":
        return lax.dot_general(a, b, dn, precision=HI, preferred_element_type=F32)
    dot = lambda x, y: lax.dot_general(x, y, dn, preferred_element_type=F32)
    ah = a.astype(BF16)
    bh = b.astype(BF16)
    if mode == "b1":
        return dot(ah, bh)
    al = (a - ah.astype(F32)).astype(BF16)
    bl = (b - bh.astype(F32)).astype(BF16)
    return dot(ah, bh) + (dot(ah, bl) + dot(al, bh))


def _rms(x, g):
    return x * lax.rsqrt(jnp.mean(x * x, axis=-1, keepdims=True) + RMS_EPS) * g


def _sigmoid(x):
    return 1.0 / (1.0 + jnp.exp(-x))


def _ffn_body(x_ref, g_ref, wg_ref, wu_ref, wd_ref, o_ref):
    x = x_ref[...]
    n = _rms(x, g_ref[...]).astype(BF16)
    acc = jnp.zeros(x.shape, F32)
    for c in range(wg_ref.shape[1] // FFN_COLS):
        sl = slice(c * FFN_COLS, (c + 1) * FFN_COLS)
        hg = jnp.dot(n, wg_ref[:, sl], preferred_element_type=F32)
        hu = jnp.dot(n, wu_ref[:, sl], preferred_element_type=F32)
        h = (hg * _sigmoid(hg) * hu).astype(BF16)
        acc = acc + jnp.dot(h, wd_ref[sl, :], preferred_element_type=F32)
    o_ref[...] = x + 0.5 * acc


def _inproj_body(x_ref, g_ref, wc_ref, wr_ref, wm_ref, pc_ref, pr_ref, pm_ref):
    n = _rms(x_ref[...], g_ref[...]).astype(BF16)
    pc_ref[...] = jnp.dot(n, wc_ref[...], preferred_element_type=F32)
    pr_ref[...] = jnp.dot(n, wr_ref[...], preferred_element_type=F32)
    pm_ref[...] = jnp.dot(n, wm_ref[...], preferred_element_type=F32)


def _outproj_body(x_ref, yc_ref, yr_ref, ym_ref, wc_ref, wr_ref, wm_ref, o_ref):
    acc = jnp.dot(yc_ref[...].astype(BF16), wc_ref[...], preferred_element_type=F32)
    acc = acc + jnp.dot(yr_ref[...].astype(BF16), wr_ref[...], preferred_element_type=F32)
    acc = acc + jnp.dot(ym_ref[...].astype(BF16), wm_ref[...], preferred_element_type=F32)
    o_ref[...] = x_ref[...] + acc


def _norm_body(x_ref, g_ref, o_ref):
    o_ref[...] = _rms(x_ref[...], g_ref[...])


def _rows(tm, width):
    return pl.BlockSpec((tm, width), lambda i: (i, 0))


def _resident(shape, *lead):
    nlead = len(lead)
    zeros = (0,) * len(shape)
    return pl.BlockSpec((None,) * nlead + tuple(shape), lambda i: tuple(lead) + zeros,
                        pipeline_mode=pl.Buffered(1))


def _token_params():
    return pltpu.CompilerParams(dimension_semantics=("parallel",), vmem_limit_bytes=VMEM_LIMIT)


def _ffn(x2, W, l, which):
    n, d = x2.shape
    f = W["ffn_gate"].shape[-1]
    tm = min(TOKEN_TILE, n)
    return pl.pallas_call(
        _ffn_body,
        grid=(n // tm,),
        in_specs=[_rows(tm, d), _resident((1, d), l, which), _resident((d, f), l, which),
                  _resident((d, f), l, which), _resident((f, d), l, which)],
        out_specs=_rows(tm, d),
        out_shape=jax.ShapeDtypeStruct((n, d), F32),
        compiler_params=_token_params(),
        name="ffn",
    )(x2, W["norm_ff"], W["ffn_gate"], W["ffn_up"], W["ffn_down"])


def _inproj(x2, W, l):
    n, d = x2.shape
    tm = min(TOKEN_TILE, n)
    wc, wr, wm = W["w_in_c"], W["w_in_r"], W["w_in_m"]
    widths = (wc.shape[-1], wr.shape[-1], wm.shape[-1])
    return pl.pallas_call(
        _inproj_body,
        grid=(n // tm,),
        in_specs=[_rows(tm, d), _resident((1, d), l)] + [_resident((d, w), l) for w in widths],
        out_specs=[_rows(tm, w) for w in widths],
        out_shape=[jax.ShapeDtypeStruct((n, w), F32) for w in widths],
        compiler_params=_token_params(),
        name="inproj",
    )(x2, W["norm_mix"], wc, wr, wm)


def _outproj(x2, yc, yr, ym, W, l):
    n, d = x2.shape
    tm = min(TOKEN_TILE, n)
    ws = (W["w_out_c"], W["w_out_r"], W["w_out_m"])
    return pl.pallas_call(
        _outproj_body,
        grid=(n // tm,),
        in_specs=[_rows(tm, d)] + [_rows(tm, w.shape[1]) for w in ws]
                 + [_resident(w.shape[1:], l) for w in ws],
        out_specs=_rows(tm, d),
        out_shape=jax.ShapeDtypeStruct((n, d), F32),
        compiler_params=_token_params(),
        name="outproj",
    )(x2, yc, yr, ym, *ws)


def _final_norm(x2, g):
    n, d = x2.shape
    tm = min(TOKEN_TILE, n)
    return pl.pallas_call(
        _norm_body,
        grid=(n // tm,),
        in_specs=[_rows(tm, d), pl.BlockSpec((1, d), lambda i: (0, 0))],
        out_specs=_rows(tm, d),
        out_shape=jax.ShapeDtypeStruct((n, d), F32),
        compiler_params=_token_params(),
        name="final_norm",
    )(x2, g)


def _conv_body(pc_ref, buf_ref, w_ref, y_ref, nb_ref, xp_scr, *, L, lv):
    c = pl.program_id(1)
    hist = CONV_W - 1

    @pl.when(c == 0)
    def _():
        xp_scr[8 - hist:8, :] = buf_ref[...]

    p = pc_ref[...]
    b_gate = p[:, :D_CONV]
    xp_scr[8:8 + L, :] = p[:, D_CONV:2 * D_CONV] * p[:, 2 * D_CONV:]
    w = w_ref[...]
    z = w[0:1] * xp_scr[6:6 + L, :]
    z = z + w[1:2] * xp_scr[7:7 + L, :]
    z = z + w[2:3] * xp_scr[8:8 + L, :]
    y_ref[...] = b_gate * z

    @pl.when(c == pl.num_programs(1) - 1)
    def _():
        nb_ref[...] = xp_scr[8 + lv - hist:8 + lv, :]

    tail = xp_scr[8 + L - hist:8 + L, :]
    xp_scr[8 - hist:8, :] = tail


def _softplus(x):
    return jnp.maximum(x, 0.0) + jnp.log1p(jnp.exp(-jnp.abs(x)))


def _rwkv_body(pr_ref, sh_ref, s0_ref, mu_ref, w0_ref, wup_ref, a0_ref, aup_ref, gup_ref, kk_ref, ka_ref,
               rk_ref, lnw_ref, lnb_ref, y_ref, sho_ref, so_ref, xp_scr, s_scr, *, L, lv):
    c = pl.program_id(1)

    @pl.when(c == 0)
    def _():
        xp_scr[7:8, :] = sh_ref[...]
        s_scr[...] = s0_ref[...]

    p = pr_ref[...]
    xp_scr[8:8 + L, :] = p
    prev = xp_scr[7:7 + L, :]
    ps = p + (prev - p) * mu_ref[...]
    r = ps[:, 0:D_RWKV]
    k = ps[:, D_RWKV:2 * D_RWKV]
    v = ps[:, 2 * D_RWKV:3 * D_RWKV]
    x4 = ps[:, 3 * D_RWKV:]
    w_log = -_softplus(-(w0_ref[...] + _mm(jnp.tanh(x4), wup_ref[...], NN, MODE_PROJ))) - 0.5
    lw = -jnp.exp(w_log)
    a = _sigmoid(a0_ref[...] + _mm(x4, aup_ref[...], NN, MODE_PROJ))
    g = _mm(_sigmoid(x4), gup_ref[...], NN, MODE_PROJ)

    lane = lax.broadcasted_iota(jnp.int32, (1, LANES), 1)
    lo = lane < HD_R

    def pair_sum(x):
        s0 = jnp.sum(jnp.where(lo, x, 0.0), axis=-1, keepdims=True)
        s1 = jnp.sum(jnp.where(lo, 0.0, x), axis=-1, keepdims=True)
        return jnp.where(lo, s0, s1)

    def head_sum(x):
        return jnp.concatenate([pair_sum(x[:, j * LANES:(j + 1) * LANES]) for j in range(N_PAIR)], axis=1)

    kk = k * kk_ref[...]
    kk = kk * lax.rsqrt(jnp.maximum(head_sum(kk * kk), 1e-24))
    k2 = k * (1.0 + (a - 1.0) * ka_ref[...])
    bonus = head_sum(r * k2 * rk_ref[...]) * v

    if lv < L:
        valid = lax.broadcasted_iota(jnp.int32, (L, 1), 0) < lv
        lw = jnp.where(valid, lw, 0.0)
        kk = jnp.where(valid, kk, 0.0)
        k2 = jnp.where(valid, k2, 0.0)

    ri = lax.broadcasted_iota(jnp.int32, (L, L), 0)
    ci = lax.broadcasted_iota(jnp.int32, (L, L), 1)
    cl = _mm((ci <= ri).astype(F32), lw, NN, "hi")
    e_pos = jnp.exp(cl)
    e_neg = jnp.exp(-cl)
    a_t = -kk * jnp.exp(cl - lw)
    b_t = kk * a * e_neg
    k_t = k2 * e_neg
    r_t = r * e_pos
    g_last = e_pos[L - 1:L, :]

    L2 = 2 * L
    ri2 = lax.broadcasted_iota(jnp.int32, (L2, L2), 0)
    ci2 = lax.broadcasted_iota(jnp.int32, (L2, L2), 1)
    tri2 = ri2 & (L - 1)
    tci2 = ci2 & (L - 1)
    strict = tci2 < tri2
    incl = tci2 <= tri2
    eye = (ri2 == ci2).astype(F32)
    row_lo = lax.broadcasted_iota(jnp.int32, (L2, 1), 0) < L
    own = jnp.logical_and(row_lo, lo) | jnp.logical_and(jnp.logical_not(row_lo), jnp.logical_not(lo))

    def stack(x):
        return jnp.concatenate([jnp.where(lo, x, 0.0), jnp.where(lo, 0.0, x)], axis=0)

    outs = []
    for j in range(N_PAIR):
        sl = slice(j * LANES, (j + 1) * LANES)
        xa, xb, xk, xr, xv = (stack(t[:, sl]) for t in (a_t, b_t, k_t, r_t, v))
        s_bd = s_scr[j]
        xar = jnp.concatenate([xa, xr], axis=0)
        gram = _mm(xar, jnp.concatenate([xb, xk], axis=0), NT, MODE_SCAN)
        a_ab = jnp.where(strict, gram[:L2, :L2], 0.0)
        a_ak = jnp.where(strict, gram[:L2, L2:], 0.0)
        m_rb = jnp.where(incl, gram[L2:, :L2], 0.0)
        m_rk = jnp.where(incl, gram[L2:, L2:], 0.0)
        tinv = eye + a_ab
        pw = a_ab
        for _ in range(L.bit_length() - 2):
            pw = _mm(pw, pw, NN, MODE_SCAN)
            tinv = tinv + _mm(tinv, pw, NN, MODE_SCAN)
        sh = _mm(xar, s_bd, NT, MODE_SCAN)
        u = _mm(tinv, sh[:L2] + _mm(a_ak, xv, NN, MODE_SCAN), NN, MODE_SCAN)
        o = sh[L2:] + _mm(m_rb, u, NN, MODE_SCAN) + _mm(m_rk, xv, NN, MODE_SCAN)
        s_new = s_bd + _mm(u, xb, TN, MODE_SCAN) + _mm(xv, xk, TN, MODE_SCAN)
        s_scr[j] = s_new * g_last[:, sl]
        mean = jnp.sum(o, axis=-1, keepdims=True) * (1.0 / HD_R)
        oc = jnp.where(own, o - mean, 0.0)
        var = jnp.sum(oc * oc, axis=-1, keepdims=True) * (1.0 / HD_R)
        on = oc * lax.rsqrt(var + GN_EPS_R)
        outs.append(on[:L] + on[L:])

    yn = jnp.concatenate(outs, axis=1) * lnw_ref[...] + lnb_ref[...]
    y_ref[...] = (yn + bonus) * g

    @pl.when(c == pl.num_programs(1) - 1)
    def _():
        sho_ref[...] = xp_scr[8 + lv - 1:8 + lv, :]
        so_ref[...] = s_scr[...]

    xp_scr[7:8, :] = xp_scr[7 + L:8 + L, :]


def _mlstm_body(pm_ref, cb_ref, c0_ref, n0_ref, m0_ref, cw_ref, cbias_ref, wq_ref, wk_ref, wv_ref,
                wgq_ref, wgk_ref, wgv_ref, bg_ref, gnw_ref, skip_ref,
                y_ref, cbo_ref, co_ref, no_ref, mo_ref, xp_scr, c_scr, n_scr, m_scr, *, L, lv):
    c = pl.program_id(1)
    hist = MCONV_W - 1

    @pl.when(c == 0)
    def _():
        xp_scr[8 - hist:8, :] = cb_ref[...]
        c_scr[...] = c0_ref[...]
        n_scr[...] = n0_ref[...]
        m_scr[...] = m0_ref[...]

    pm = pm_ref[...]
    u = pm[:, :D_MPAD]
    og = pm[:, D_MPAD:]
    xp_scr[8:8 + L, :] = u
    cw = cw_ref[...]
    uc = cw[0:1] * xp_scr[5:5 + L, :]
    uc = uc + cw[1:2] * xp_scr[6:6 + L, :]
    uc = uc + cw[2:3] * xp_scr[7:7 + L, :]
    uc = uc + cw[3:4] * u
    uc = uc + cbias_ref[...]
    uc = uc * _sigmoid(uc)
    q = _mm(uc, wq_ref[...], NN, MODE_PROJ)
    k = _mm(uc, wk_ref[...], NN, MODE_PROJ)
    v = _mm(u, wv_ref[...], NN, MODE_PROJ)
    gp = (_mm(q, wgq_ref[...], NN, MODE_PROJ) + _mm(k, wgk_ref[...], NN, MODE_PROJ)
          + _mm(v, wgv_ref[...], NN, MODE_PROJ)) + bg_ref[...]
    k = k * (HD_M ** -0.5)

    ig = gp
    lf = -_softplus(-gp)
    if lv < L:
        valid = lax.broadcasted_iota(jnp.int32, (L, 1), 0) < lv
        ig = jnp.where(valid, ig, -jnp.inf)
        lf = jnp.where(valid, lf, 0.0)
    ri = lax.broadcasted_iota(jnp.int32, (L, L), 0)
    ci = lax.broadcasted_iota(jnp.int32, (L, L), 1)
    causal = ci <= ri
    bcum = _mm(causal.astype(F32), lf, NN, "hi")
    lane = lax.broadcasted_iota(jnp.int32, (1, LANES), 1)
    gates = jnp.where(lane < H_M, ig, bcum)
    gates_t = gates.T
    head_lane = lane < HD_M

    outs = []
    for h in range(H_M):
        sl = slice(h * LANES, (h + 1) * LANES)
        qh, kh, vh = q[:, sl], k[:, sl], v[:, sl]
        i_c = gates[:, h:h + 1]
        b_c = gates[:, H_M + h:H_M + h + 1]
        i_r = gates_t[h:h + 1, :]
        b_r = gates_t[H_M + h:H_M + h + 1, :]
        m_prev = m_scr[h:h + 1, 0:1]
        c_h = c_scr[h]
        n_h = n_scr[h:h + 1, :]
        dm = jnp.where(causal, b_c - b_r + i_r, -jnp.inf)
        inter = b_c + m_prev
        m_t = jnp.maximum(inter, jnp.max(dm, axis=-1, keepdims=True))
        s = _mm(qh, kh, NT, MODE_SCAN) * jnp.exp(dm - m_t)
        w_i = jnp.exp(inter - m_t)
        num = _mm(s, vh, NN, MODE_SCAN) + w_i * _mm(qh, c_h, NT, MODE_SCAN)
        den = jnp.sum(s, axis=-1, keepdims=True) + w_i * jnp.sum(qh * n_h, axis=-1, keepdims=True)
        hh = num / jnp.maximum(jnp.abs(den), jnp.exp(-m_t))
        b_l = b_c[L - 1:L]
        m_new = m_t[L - 1:L]
        w_s = jnp.exp(b_l - b_c + i_c - m_new)
        w_p = jnp.exp(b_l + m_prev - m_new)
        ks = w_s * kh
        c_scr[h] = w_p * c_h + _mm(vh, ks, TN, MODE_SCAN)
        n_scr[h:h + 1, :] = w_p * n_h + jnp.sum(ks, axis=0, keepdims=True)
        m_scr[h:h + 1, :] = jnp.broadcast_to(m_new, (1, LANES))
        mean = jnp.sum(hh, axis=-1, keepdims=True) * (1.0 / HD_M)
        hc = jnp.where(head_lane, hh - mean, 0.0)
        var = jnp.sum(hc * hc, axis=-1, keepdims=True) * (1.0 / HD_M)
        outs.append(hc * lax.rsqrt(var + GN_EPS_M))

    y = jnp.concatenate(outs, axis=1) * gnw_ref[...] + skip_ref[...] * uc
    y_ref[...] = y * _sigmoid(og)

    @pl.when(c == pl.num_programs(1) - 1)
    def _():
        cbo_ref[...] = xp_scr[8 + lv - hist:8 + lv, :]
        co_ref[...] = c_scr[...]
        no_ref[...] = n_scr[...]
        mo_ref[...] = m_scr[...]

    tail = xp_scr[8 + L - hist:8 + L, :]
    xp_scr[8 - hist:8, :] = tail


def _seq_spec(L, width):
    return pl.BlockSpec((None, L, width), lambda b, c: (b, c, 0))


def _state_spec(shape, l):
    zeros = (0,) * len(shape)
    return pl.BlockSpec((None, None) + tuple(shape), lambda b, c: (l, b) + zeros)


def _state_out_spec(shape):
    zeros = (0,) * len(shape)
    return pl.BlockSpec((None,) + tuple(shape), lambda b, c: (b,) + zeros)


def _param_spec(shape, l):
    zeros = (0,) * len(shape)
    return pl.BlockSpec((None,) + tuple(shape), lambda b, c: (l,) + zeros)


def _mixer_params():
    return pltpu.CompilerParams(dimension_semantics=("parallel", "arbitrary"), vmem_limit_bytes=VMEM_LIMIT)


def _chunking(T, t_valid, chunk):
    L = min(T, chunk)
    nc = T // L
    lv = t_valid - (nc - 1) * L
    assert T % L == 0 and 0 < lv <= L and (lv == L or nc == 1)
    return L, nc, lv


def _conv_mixer(pc, st, W, l, t_valid):
    B, T, _ = pc.shape
    L, nc, lv = _chunking(T, t_valid, CONV_CHUNK)
    hist = CONV_W - 1
    return pl.pallas_call(
        functools.partial(_conv_body, L=L, lv=lv),
        grid=(B, nc),
        in_specs=[_seq_spec(L, 3 * D_CONV), _state_spec((hist, D_CONV), l), _param_spec((CONV_W, D_CONV), l)],
        out_specs=[_seq_spec(L, D_CONV), _state_out_spec((hist, D_CONV))],
        out_shape=[jax.ShapeDtypeStruct((B, T, D_CONV), F32), jax.ShapeDtypeStruct((B, hist, D_CONV), F32)],
        scratch_shapes=[pltpu.VMEM((8 + L, D_CONV), F32)],
        compiler_params=_mixer_params(),
        name="short_conv",
    )(pc, st["conv"], W["conv_w"])


def _rwkv_mixer(pr, st, W, l, t_valid):
    B, T, _ = pr.shape
    L, nc, lv = _chunking(T, t_valid, MIX_CHUNK)
    vec = lambda n: _param_spec((1, n), l)
    return pl.pallas_call(
        functools.partial(_rwkv_body, L=L, lv=lv),
        grid=(B, nc),
        in_specs=[_seq_spec(L, P_RWKV), _state_spec((1, P_RWKV), l), _state_spec((N_PAIR, LANES, LANES), l),
                  vec(P_RWKV), vec(D_RWKV), _param_spec((LANES, D_RWKV), l), vec(D_RWKV),
                  _param_spec((LANES, D_RWKV), l), _param_spec((LANES, D_RWKV), l),
                  vec(D_RWKV), vec(D_RWKV), vec(D_RWKV), vec(D_RWKV), vec(D_RWKV)],
        out_specs=[_seq_spec(L, D_RWKV), _state_out_spec((1, P_RWKV)), _state_out_spec((N_PAIR, LANES, LANES))],
        out_shape=[jax.ShapeDtypeStruct((B, T, D_RWKV), F32), jax.ShapeDtypeStruct((B, 1, P_RWKV), F32),
                   jax.ShapeDtypeStruct((B, N_PAIR, LANES, LANES), F32)],
        scratch_shapes=[pltpu.VMEM((8 + L, P_RWKV), F32), pltpu.VMEM((N_PAIR, LANES, LANES), F32)],
        compiler_params=_mixer_params(),
        name="rwkv7",
    )(pr, st["shift"], st["wkv"], W["rwkv_mu"], W["rwkv_w0"], W["rwkv_w_up"], W["rwkv_a0"], W["rwkv_a_up"],
      W["rwkv_g_up"], W["rwkv_k_k"], W["rwkv_k_a"], W["rwkv_r_k"], W["rwkv_ln_w"], W["rwkv_ln_b"])


def _mlstm_mixer(pm, st, W, l, t_valid):
    B, T, _ = pm.shape
    L, nc, lv = _chunking(T, t_valid, MIX_CHUNK)
    hist = MCONV_W - 1
    vec = lambda n: _param_spec((1, n), l)
    sq = _param_spec((D_MPAD, D_MPAD), l)
    gate = _param_spec((D_MPAD, LANES), l)
    return pl.pallas_call(
        functools.partial(_mlstm_body, L=L, lv=lv),
        grid=(B, nc),
        in_specs=[_seq_spec(L, 2 * D_MPAD), _state_spec((hist, D_MPAD), l), _state_spec((H_M, LANES, LANES), l),
                  _state_spec((H_M, LANES), l), _state_spec((H_M, LANES), l),
                  _param_spec((MCONV_W, D_MPAD), l), vec(D_MPAD), sq, sq, sq, gate, gate, gate, vec(LANES),
                  vec(D_MPAD), vec(D_MPAD)],
        out_specs=[_seq_spec(L, D_MPAD), _state_out_spec((hist, D_MPAD)), _state_out_spec((H_M, LANES, LANES)),
                   _state_out_spec((H_M, LANES)), _state_out_spec((H_M, LANES))],
        out_shape=[jax.ShapeDtypeStruct((B, T, D_MPAD), F32), jax.ShapeDtypeStruct((B, hist, D_MPAD), F32),
                   jax.ShapeDtypeStruct((B, H_M, LANES, LANES), F32), jax.ShapeDtypeStruct((B, H_M, LANES), F32),
                   jax.ShapeDtypeStruct((B, H_M, LANES), F32)],
        scratch_shapes=[pltpu.VMEM((8 + L, D_MPAD), F32), pltpu.VMEM((H_M, LANES, LANES), F32),
                        pltpu.VMEM((H_M, LANES), F32), pltpu.VMEM((H_M, LANES), F32)],
        compiler_params=_mixer_params(),
        name="mlstm",
    )(pm, st["mconv"], st["mC"], st["mn"], st["mm"], W["mlstm_conv_w"], W["mlstm_conv_b"], W["mlstm_wq"],
      W["mlstm_wk"], W["mlstm_wv"], W["mlstm_wgq"], W["mlstm_wgk"], W["mlstm_wgv"], W["mlstm_b_gate"],
      W["mlstm_gn_w"], W["mlstm_skip"])


def _pad_heads(x, axis):
    axis = axis % x.ndim
    shp = x.shape
    x = x.reshape(shp[:axis] + (H_M, HD_M) + shp[axis + 1:])
    pad = [(0, 0)] * x.ndim
    pad[axis + 1] = (0, LANES - HD_M)
    x = jnp.pad(x, pad)
    return x.reshape(shp[:axis] + (D_MPAD,) + shp[axis + 1:])


def _unpad_heads(x, axis):
    axis = axis % x.ndim
    shp = x.shape
    x = x.reshape(shp[:axis] + (H_M, LANES) + shp[axis + 1:])
    x = lax.slice_in_dim(x, 0, HD_M, axis=axis + 1)
    return x.reshape(shp[:axis] + (D_MLSTM,) + shp[axis + 1:])


def _head_block_diag(w):
    wp = jnp.pad(w, ((0, 0), (0, 0), (0, LANES - HD_M), (0, LANES - HD_M)))
    eye = jnp.eye(H_M, dtype=w.dtype)
    return jnp.einsum("lhde,hg->lhdge", wp, eye).reshape(w.shape[0], D_MPAD, D_MPAD)


def _prep_weights(norm_ff, ffn_gate, ffn_up, ffn_down, norm_mix, w_in, w_out, conv_w,
                  rwkv_mu, rwkv_w0, rwkv_w_up, rwkv_a0, rwkv_a_up, rwkv_g_up, rwkv_k_k, rwkv_k_a, rwkv_r_k,
                  rwkv_ln_w, rwkv_ln_b, mlstm_conv_w, mlstm_conv_b, mlstm_wq, mlstm_wk, mlstm_wv,
                  mlstm_w_gate, mlstm_b_gate, mlstm_gn_w, mlstm_skip, norm_final):
    depth = w_in.shape[0]
    pc, pr = 3 * D_CONV, P_RWKV
    row = lambda x: x[:, None, :]
    W = {}
    W["norm_ff"] = norm_ff[:, :, None, :]
    W["ffn_gate"] = ffn_gate.astype(BF16)
    W["ffn_up"] = ffn_up.astype(BF16)
    W["ffn_down"] = ffn_down.astype(BF16)
    W["norm_mix"] = row(norm_mix)
    W["w_in_c"] = w_in[:, :, :pc].astype(BF16)
    W["w_in_r"] = w_in[:, :, pc:pc + pr].astype(BF16)
    w_u = _pad_heads(w_in[:, :, pc + pr:pc + pr + D_MLSTM], 2)
    w_o = _pad_heads(w_in[:, :, pc + pr + D_MLSTM:], 2)
    W["w_in_m"] = jnp.concatenate([w_u, w_o], axis=2).astype(BF16)
    W["w_out_c"] = w_out[:, :D_CONV].astype(BF16)
    W["w_out_r"] = w_out[:, D_CONV:D_CONV + D_RWKV].astype(BF16)
    W["w_out_m"] = _pad_heads(w_out[:, D_CONV + D_RWKV:], 1).astype(BF16)
    W["conv_w"] = conv_w
    W["rwkv_mu"] = row(rwkv_mu)
    W["rwkv_w0"] = row(rwkv_w0)
    W["rwkv_a0"] = row(rwkv_a0)
    zl = lambda n: jnp.zeros((depth, n, D_RWKV), F32)
    W["rwkv_w_up"] = jnp.concatenate([rwkv_w_up, zl(LANES - LORA_W)], axis=1)
    W["rwkv_a_up"] = jnp.concatenate([zl(LORA_W), rwkv_a_up, zl(LORA_G)], axis=1)
    W["rwkv_g_up"] = jnp.concatenate([zl(LORA_W + LORA_A), rwkv_g_up], axis=1)
    W["rwkv_k_k"] = row(rwkv_k_k)
    W["rwkv_k_a"] = row(rwkv_k_a)
    W["rwkv_r_k"] = row(rwkv_r_k.reshape(depth, D_RWKV))
    W["rwkv_ln_w"] = row(rwkv_ln_w)
    W["rwkv_ln_b"] = row(rwkv_ln_b)
    W["mlstm_conv_w"] = _pad_heads(mlstm_conv_w, 2)
    W["mlstm_conv_b"] = row(_pad_heads(mlstm_conv_b, 1))
    W["mlstm_wq"] = _head_block_diag(mlstm_wq)
    W["mlstm_wk"] = _head_block_diag(mlstm_wk)
    W["mlstm_wv"] = _head_block_diag(mlstm_wv)
    wg = mlstm_w_gate.reshape(depth, H_M, 3, HD_M, 2 * H_M)
    for i, name in enumerate(("mlstm_wgq", "mlstm_wgk", "mlstm_wgv")):
        part = jnp.pad(wg[:, :, i], ((0, 0), (0, 0), (0, LANES - HD_M), (0, LANES - 2 * H_M)))
        W[name] = part.reshape(depth, D_MPAD, LANES)
    W["mlstm_b_gate"] = row(jnp.pad(mlstm_b_gate, ((0, 0), (0, LANES - 2 * H_M))))
    W["mlstm_gn_w"] = row(_pad_heads(mlstm_gn_w, 1))
    W["mlstm_skip"] = row(_pad_heads(mlstm_skip, 1))
    W["norm_final"] = norm_final[None, :]
    return W


def _wkv_to_pairs(s):
    d, b = s.shape[:2]
    s = s.reshape(d, b, N_PAIR, 2, HD_R, HD_R)
    eye = jnp.eye(2, dtype=s.dtype)
    return jnp.einsum("dbpivk,ij->dbpivjk", s, eye).reshape(d, b, N_PAIR, LANES, LANES)


def _wkv_from_pairs(s):
    d, b = s.shape[:2]
    s = s.reshape(d, b, N_PAIR, 2, HD_R, 2, HD_R)
    s = jnp.stack([s[:, :, :, 0, :, 0, :], s[:, :, :, 1, :, 1, :]], axis=3)
    return s.reshape(d, b, H_R, HD_R, HD_R)


def _prep_states(conv, shift, wkv, mconv, mC, mn, mm):
    pad_c = ((0, 0), (0, 0), (0, 0), (0, LANES - HD_M), (0, LANES - HD_M))
    return {
        "conv": conv,
        "shift": shift,
        "wkv": _wkv_to_pairs(wkv),
        "mconv": _pad_heads(mconv, 3),
        "mC": jnp.pad(mC, pad_c),
        "mn": jnp.pad(mn, ((0, 0), (0, 0), (0, 0), (0, LANES - HD_M))),
        "mm": jnp.broadcast_to(mm[..., None], mm.shape + (LANES,)),
    }


def _finish_states(outs):
    conv, shift, wkv, mconv, mC, mn, mm = (jnp.stack(o) for o in outs)
    return (conv, shift, _wkv_from_pairs(wkv), _unpad_heads(mconv, 3), mC[..., :HD_M, :HD_M],
            mn[..., :HD_M], mm[..., 0])


def _run_trunk(x, st, W, t_valid):
    B, T, D = x.shape
    depth = W["w_in_c"].shape[0]
    x2 = x.reshape(B * T, D)
    outs = [[] for _ in range(7)]
    seq = lambda a: a.reshape(B, T, a.shape[-1])
    for l in range(depth):
        x2 = _ffn(x2, W, l, 0)
        pc, pr, pm = _inproj(x2, W, l)
        yc, c_new = _conv_mixer(seq(pc), st, W, l, t_valid)
        yr, sh_new, s_new = _rwkv_mixer(seq(pr), st, W, l, t_valid)
        ym, mc_new, C_new, n_new, m_new = _mlstm_mixer(seq(pm), st, W, l, t_valid)
        flat = lambda a: a.reshape(B * T, a.shape[-1])
        x2 = _outproj(x2, flat(yc), flat(yr), flat(ym), W, l)
        x2 = _ffn(x2, W, l, 1)
        for lst, s in zip(outs, (c_new, sh_new, s_new, mc_new, C_new, n_new, m_new)):
            lst.append(s)
    y = _final_norm(x2, W["norm_final"]).reshape(B, T, D)
    return y, _finish_states(outs)


def kernel(x_prompt, x_sample, cache_conv, cache_shift, state_wkv, cache_mconv, state_mC, state_mn, state_mm, norm_ff, ffn_gate, ffn_up, ffn_down, norm_mix, w_in, w_out, conv_w, rwkv_mu, rwkv_w0, rwkv_w_up, rwkv_a0, rwkv_a_up, rwkv_g_up, rwkv_k_k, rwkv_k_a, rwkv_r_k, rwkv_ln_w, rwkv_ln_b, mlstm_conv_w, mlstm_conv_b, mlstm_wq, mlstm_wk, mlstm_wv, mlstm_w_gate, mlstm_b_gate, mlstm_gn_w, mlstm_skip, norm_final):
    W = _prep_weights(norm_ff, ffn_gate, ffn_up, ffn_down, norm_mix, w_in, w_out, conv_w,
                      rwkv_mu, rwkv_w0, rwkv_w_up, rwkv_a0, rwkv_a_up, rwkv_g_up, rwkv_k_k, rwkv_k_a, rwkv_r_k,
                      rwkv_ln_w, rwkv_ln_b, mlstm_conv_w, mlstm_conv_b, mlstm_wq, mlstm_wk, mlstm_wv,
                      mlstm_w_gate, mlstm_b_gate, mlstm_gn_w, mlstm_skip, norm_final)
    depth = w_in.shape[0]
    Bp, Tp, _ = x_prompt.shape
    Bs, Ts, _ = x_sample.shape
    zeros = lambda *shape: jnp.zeros((depth, Bp) + shape, F32)
    st_p = _prep_states(zeros(CONV_W - 1, D_CONV), zeros(1, P_RWKV), zeros(H_R, HD_R, HD_R),
                        zeros(MCONV_W - 1, D_MLSTM), zeros(H_M, HD_M, HD_M), zeros(H_M, HD_M), zeros(H_M))
    y_prompt, p_states = _run_trunk(x_prompt, st_p, W, Tp)

    st_s = _prep_states(cache_conv, cache_shift, state_wkv, cache_mconv, state_mC, state_mn, state_mm)
    xs = jnp.pad(x_sample, ((0, 0), (0, SAMPLE_PAD_T - Ts), (0, 0)))
    y_sample, s_states = _run_trunk(xs, st_s, W, Ts)
    y_sample = y_sample[:, :Ts]
    return (y_prompt, y_sample) + tuple(p_states) + tuple(s_states)
```

```python
import functools

import jax
import jax.numpy as jnp
from jax import lax
from jax.experimental import pallas as pl
from jax.experimental.pallas import tpu as pltpu

F32 = jnp.float32
BF16 = jnp.bfloat16
HI = lax.Precision.HIGHEST

LANES = 128
H_R, HD_R = 6, 64
D_RWKV = H_R * HD_R
N_PAIR = H_R // 2
LORA_W, LORA_A, LORA_G = 32, 32, 64
P_RWKV = 3 * D_RWKV + LORA_W + LORA_A + LORA_G
H_M, HD_M = 4, 96
D_MLSTM = H_M * HD_M
D_MPAD = H_M * LANES
D_CONV = 256
CONV_W, MCONV_W = 3, 4
RMS_EPS = 1e-6
GN_EPS_R = 64e-5
GN_EPS_M = 1e-6
SAMPLE_PAD_T = 8
RWKV_CHUNK = 64
RWKV_STEP_ROWS = 128
MLSTM_CHUNK = 256
MLSTM_STEP_ROWS = 256
CONV_CHUNK = 512
MAX_SEQS_PER_STEP = 8
TOKEN_TILE = 512
FFN_COLS = 256
VMEM_LIMIT = 48 * 1024 * 1024

NN = ((1,), (0,))
NT = ((1,), (1,))
TN = ((0,), (0,))

MODE_PROJ = "b1"
MODE_SCAN = "b1"


def _mm(a, b, dims=NN, mode="b1"):
    dn = (dims, ((), ()))
    if mode == "hi":
        return lax.dot_general(a, b, dn, precision=HI, preferred_element_type=F32)
    dot = lambda x, y: lax.dot_general(x, y, dn, preferred_element_type=F32)
    ah = a.astype(BF16)
    bh = b.astype(BF16)
    if mode == "b1":
        return dot(ah, bh)
    al = (a - ah.astype(F32)).astype(BF16)
    bl = (b - bh.astype(F32)).astype(BF16)
    return dot(ah, bh) + (dot(ah, bl) + dot(al, bh))


def _rms(x, g):
    return x * lax.rsqrt(jnp.mean(x * x, axis=-1, keepdims=True) + RMS_EPS) * g


def _sigmoid(x):
    return 1.0 / (1.0 + jnp.exp(-x))


def _softplus(x):
    return jnp.maximum(x, 0.0) + jnp.log1p(jnp.exp(-jnp.abs(x)))


def _cat_rows(xs):
    return xs[0] if len(xs) == 1 else jnp.concatenate(xs, axis=0)


def _ffn_body(x_ref, g_ref, wg_ref, wu_ref, wd_ref, o_ref):
    x = x_ref[...]
    n = _rms(x, g_ref[...]).astype(BF16)
    acc = jnp.zeros(x.shape, F32)
    for c in range(wg_ref.shape[1] // FFN_COLS):
        sl = slice(c * FFN_COLS, (c + 1) * FFN_COLS)
        hg = jnp.dot(n, wg_ref[:, sl], preferred_element_type=F32)
        hu = jnp.dot(n, wu_ref[:, sl], preferred_element_type=F32)
        h = (hg * _sigmoid(hg) * hu).astype(BF16)
        acc = acc + jnp.dot(h, wd_ref[sl, :], preferred_element_type=F32)
    o_ref[...] = x + 0.5 * acc


def _inproj_body(x_ref, g_ref, wc_ref, wr_ref, wm_ref, pc_ref, pr_ref, pm_ref):
    n = _rms(x_ref[...], g_ref[...]).astype(BF16)
    pc_ref[...] = jnp.dot(n, wc_ref[...], preferred_element_type=F32)
    pr_ref[...] = jnp.dot(n, wr_ref[...], preferred_element_type=F32)
    pm_ref[...] = jnp.dot(n, wm_ref[...], preferred_element_type=F32)


def _outproj_body(x_ref, yc_ref, yr_ref, ym_ref, wc_ref, wr_ref, wm_ref, o_ref):
    acc = jnp.dot(yc_ref[...].astype(BF16), wc_ref[...], preferred_element_type=F32)
    acc = acc + jnp.dot(yr_ref[...].astype(BF16), wr_ref[...], preferred_element_type=F32)
    acc = acc + jnp.dot(ym_ref[...].astype(BF16), wm_ref[...], preferred_element_type=F32)
    o_ref[...] = x_ref[...] + acc


def _norm_body(x_ref, g_ref, o_ref):
    o_ref[...] = _rms(x_ref[...], g_ref[...])


def _rows(tm, width):
    return pl.BlockSpec((tm, width), lambda i: (i, 0))


def _resident(shape, *lead):
    nlead = len(lead)
    zeros = (0,) * len(shape)
    return pl.BlockSpec((None,) * nlead + tuple(shape), lambda i: tuple(lead) + zeros,
                        pipeline_mode=pl.Buffered(1))


def _token_params():
    return pltpu.CompilerParams(dimension_semantics=("parallel",), vmem_limit_bytes=VMEM_LIMIT)


def _ffn(x2, W, l, which):
    n, d = x2.shape
    f = W["ffn_gate"].shape[-1]
    tm = min(TOKEN_TILE, n)
    return pl.pallas_call(
        _ffn_body,
        grid=(n // tm,),
        in_specs=[_rows(tm, d), _resident((1, d), l, which), _resident((d, f), l, which),
                  _resident((d, f), l, which), _resident((f, d), l, which)],
        out_specs=_rows(tm, d),
        out_shape=jax.ShapeDtypeStruct((n, d), F32),
        compiler_params=_token_params(),
        name="ffn",
    )(x2, W["norm_ff"], W["ffn_gate"], W["ffn_up"], W["ffn_down"])


def _inproj(x2, W, l):
    n, d = x2.shape
    tm = min(TOKEN_TILE, n)
    wc, wr, wm = W["w_in_c"], W["w_in_r"], W["w_in_m"]
    widths = (wc.shape[-1], wr.shape[-1], wm.shape[-1])
    return pl.pallas_call(
        _inproj_body,
        grid=(n // tm,),
        in_specs=[_rows(tm, d), _resident((1, d), l)] + [_resident((d, w), l) for w in widths],
        out_specs=[_rows(tm, w) for w in widths],
        out_shape=[jax.ShapeDtypeStruct((n, w), F32) for w in widths],
        compiler_params=_token_params(),
        name="inproj",
    )(x2, W["norm_mix"], wc, wr, wm)


def _outproj(x2, yc, yr, ym, W, l):
    n, d = x2.shape
    tm = min(TOKEN_TILE, n)
    ws = (W["w_out_c"], W["w_out_r"], W["w_out_m"])
    return pl.pallas_call(
        _outproj_body,
        grid=(n // tm,),
        in_specs=[_rows(tm, d)] + [_rows(tm, w.shape[1]) for w in ws]
                 + [_resident(w.shape[1:], l) for w in ws],
        out_specs=_rows(tm, d),
        out_shape=jax.ShapeDtypeStruct((n, d), F32),
        compiler_params=_token_params(),
        name="outproj",
    )(x2, yc, yr, ym, *ws)


def _final_norm(x2, g):
    n, d = x2.shape
    tm = min(TOKEN_TILE, n)
    return pl.pallas_call(
        _norm_body,
        grid=(n // tm,),
        in_specs=[_rows(tm, d), pl.BlockSpec((1, d), lambda i: (0, 0))],
        out_specs=_rows(tm, d),
        out_shape=jax.ShapeDtypeStruct((n, d), F32),
        compiler_params=_token_params(),
        name="final_norm",
    )(x2, g)


def _conv_body(pc_ref, buf_ref, w_ref, y_ref, nb_ref, xp_scr, *, G, L, lv):
    c = pl.program_id(1)
    hist = CONV_W - 1
    w = w_ref[...]
    for g in range(G):
        @pl.when(c == 0)
        def _():
            xp_scr[g, 8 - hist:8, :] = buf_ref[g]

        p = pc_ref[g]
        b_gate = p[:, :D_CONV]
        xp_scr[g, 8:8 + L, :] = p[:, D_CONV:2 * D_CONV] * p[:, 2 * D_CONV:]
        z = w[0:1] * xp_scr[g, 6:6 + L, :]
        z = z + w[1:2] * xp_scr[g, 7:7 + L, :]
        z = z + w[2:3] * xp_scr[g, 8:8 + L, :]
        y_ref[g] = b_gate * z

        @pl.when(c == pl.num_programs(1) - 1)
        def _():
            nb_ref[g] = xp_scr[g, 8 + lv - hist:8 + lv, :]

        tail = xp_scr[g, 8 + L - hist:8 + L, :]
        xp_scr[g, 8 - hist:8, :] = tail


def _rwkv_body(pr_ref, sh_ref, s0_ref, mu_ref, w0_ref, wup_ref, a0_ref, aup_ref, gup_ref, kk_ref, ka_ref,
               rk_ref, lnw_ref, lnb_ref, y_ref, sho_ref, so_ref, xp_scr, s_scr, *, G, L, lv):
    c = pl.program_id(1)
    GL = G * L
    shift = L.bit_length() - 1
    seqs = range(G)

    @pl.when(c == 0)
    def _():
        for g in seqs:
            xp_scr[g, 7:8, :] = sh_ref[g]
        s_scr[...] = s0_ref[...]

    for g in seqs:
        xp_scr[g, 8:8 + L, :] = pr_ref[g]
    p = _cat_rows([pr_ref[g] for g in seqs])
    prev = _cat_rows([xp_scr[g, 7:7 + L, :] for g in seqs])
    ps = p + (prev - p) * mu_ref[...]
    r = ps[:, 0:D_RWKV]
    k = ps[:, D_RWKV:2 * D_RWKV]
    v = ps[:, 2 * D_RWKV:3 * D_RWKV]
    x4 = ps[:, 3 * D_RWKV:]
    w_log = -_softplus(-(w0_ref[...] + _mm(jnp.tanh(x4), wup_ref[...], NN, MODE_PROJ))) - 0.5
    lw = -jnp.exp(w_log)
    a = _sigmoid(a0_ref[...] + _mm(x4, aup_ref[...], NN, MODE_PROJ))
    gate = _mm(_sigmoid(x4), gup_ref[...], NN, MODE_PROJ)

    lane = lax.broadcasted_iota(jnp.int32, (1, LANES), 1)
    lo = lane < HD_R

    def pair_sum(x):
        s0 = jnp.sum(jnp.where(lo, x, 0.0), axis=-1, keepdims=True)
        s1 = jnp.sum(jnp.where(lo, 0.0, x), axis=-1, keepdims=True)
        return jnp.where(lo, s0, s1)

    def head_sum(x):
        return jnp.concatenate([pair_sum(x[:, j * LANES:(j + 1) * LANES]) for j in range(N_PAIR)], axis=1)

    kk = k * kk_ref[...]
    kk = kk * lax.rsqrt(jnp.maximum(head_sum(kk * kk), 1e-24))
    k2 = k * (1.0 + (a - 1.0) * ka_ref[...])
    bonus = head_sum(r * k2 * rk_ref[...]) * v

    if lv < L:
        valid = (lax.broadcasted_iota(jnp.int32, (GL, 1), 0) & (L - 1)) < lv
        lw = jnp.where(valid, lw, 0.0)
        kk = jnp.where(valid, kk, 0.0)
        k2 = jnp.where(valid, k2, 0.0)

    ri = lax.broadcasted_iota(jnp.int32, (GL, GL), 0)
    ci = lax.broadcasted_iota(jnp.int32, (GL, GL), 1)
    tri = jnp.logical_and(ci <= ri, (ci >> shift) == (ri >> shift))
    cl = _mm(tri.astype(F32), lw, NN, "hi")
    e_pos = jnp.exp(cl)
    e_neg = jnp.exp(-cl)
    a_t = -kk * jnp.exp(cl - lw)
    b_t = kk * a * e_neg
    k_t = k2 * e_neg
    r_t = r * e_pos

    L2 = 2 * L
    ri2 = lax.broadcasted_iota(jnp.int32, (L2, L2), 0)
    ci2 = lax.broadcasted_iota(jnp.int32, (L2, L2), 1)
    tri2 = ri2 & (L - 1)
    tci2 = ci2 & (L - 1)
    strict = tci2 < tri2
    incl = tci2 <= tri2
    eye = (ri2 == ci2).astype(F32)
    row_lo = lax.broadcasted_iota(jnp.int32, (L2, 1), 0) < L
    own = jnp.logical_and(row_lo, lo) | jnp.logical_and(jnp.logical_not(row_lo), jnp.logical_not(lo))

    def stack(x):
        return jnp.concatenate([jnp.where(lo, x, 0.0), jnp.where(lo, 0.0, x)], axis=0)

    units = [(g, j) for g in seqs for j in range(N_PAIR)]
    U = range(len(units))
    rsl = [slice(g * L, (g + 1) * L) for g, _ in units]
    lsl = [slice(j * LANES, (j + 1) * LANES) for _, j in units]
    mm = functools.partial(_mm, mode=MODE_SCAN)
    xa, xb, xk, xr, xv = ([stack(t[rsl[i], lsl[i]]) for i in U] for t in (a_t, b_t, k_t, r_t, v))
    s_bd = [s_scr[g, j] for g, j in units]
    xar = [jnp.concatenate([xa[i], xr[i]], axis=0) for i in U]
    gram = [mm(xar[i], jnp.concatenate([xb[i], xk[i]], axis=0), NT) for i in U]
    a_ab = [jnp.where(strict, gm[:L2, :L2], 0.0) for gm in gram]
    a_ak = [jnp.where(strict, gm[:L2, L2:], 0.0) for gm in gram]
    m_rb = [jnp.where(incl, gm[L2:, :L2], 0.0) for gm in gram]
    m_rk = [jnp.where(incl, gm[L2:, L2:], 0.0) for gm in gram]
    sh = [mm(xar[i], s_bd[i], NT) for i in U]
    rhs = [sh[i][:L2] + mm(a_ak[i], xv[i], NN) for i in U]
    o_part = [sh[i][L2:] + mm(m_rk[i], xv[i], NN) for i in U]
    s_part = [s_bd[i] + mm(xv[i], xk[i], TN) for i in U]
    tinv = [eye + m for m in a_ab]
    pw = a_ab
    for _ in range(L.bit_length() - 2):
        pw = [mm(m, m, NN) for m in pw]
        tinv = [t + mm(t, m, NN) for t, m in zip(tinv, pw)]
    u = [mm(tinv[i], rhs[i], NN) for i in U]
    o = [o_part[i] + mm(m_rb[i], u[i], NN) for i in U]
    for i, (g, j) in enumerate(units):
        g_last = e_pos[g * L + L - 1:g * L + L, lsl[i]]
        s_scr[g, j] = (s_part[i] + mm(u[i], xb[i], TN)) * g_last
    outs = []
    for i in U:
        mean = jnp.sum(o[i], axis=-1, keepdims=True) * (1.0 / HD_R)
        oc = jnp.where(own, o[i] - mean, 0.0)
        var = jnp.sum(oc * oc, axis=-1, keepdims=True) * (1.0 / HD_R)
        on = oc * lax.rsqrt(var + GN_EPS_R)
        outs.append(on[:L] + on[L:])

    yn = _cat_rows([jnp.concatenate(outs[g * N_PAIR:(g + 1) * N_PAIR], axis=1) for g in seqs])
    y = (yn * lnw_ref[...] + lnb_ref[...] + bonus) * gate
    for g in seqs:
        y_ref[g] = y[g * L:(g + 1) * L]

    @pl.when(c == pl.num_programs(1) - 1)
    def _():
        for g in seqs:
            sho_ref[g] = xp_scr[g, 8 + lv - 1:8 + lv, :]
        so_ref[...] = s_scr[...]

    for g in seqs:
        xp_scr[g, 7:8, :] = xp_scr[g, 7 + L:8 + L, :]


def _mlstm_body(pm_ref, cb_ref, c0_ref, n0_ref, m0_ref, cw_ref, cbias_ref, wq_ref, wk_ref, wv_ref,
                wgq_ref, wgk_ref, wgv_ref, bg_ref, gnw_ref, skip_ref,
                y_ref, cbo_ref, co_ref, no_ref, mo_ref, xp_scr, c_scr, n_scr, m_scr, *, G, L, lv):
    c = pl.program_id(1)
    hist = MCONV_W - 1
    GL = G * L
    shift = L.bit_length() - 1
    seqs = range(G)

    @pl.when(c == 0)
    def _():
        for g in seqs:
            xp_scr[g, 8 - hist:8, :] = cb_ref[g]
        c_scr[...] = c0_ref[...]
        n_scr[...] = n0_ref[...]
        m_scr[...] = m0_ref[...]

    cw = cw_ref[...]
    ucs = []
    for g in seqs:
        u_g = pm_ref[g][:, :D_MPAD]
        xp_scr[g, 8:8 + L, :] = u_g
        acc = cw[0:1] * xp_scr[g, 5:5 + L, :]
        acc = acc + cw[1:2] * xp_scr[g, 6:6 + L, :]
        acc = acc + cw[2:3] * xp_scr[g, 7:7 + L, :]
        ucs.append(acc + cw[3:4] * u_g)
    pm = _cat_rows([pm_ref[g] for g in seqs])
    u = pm[:, :D_MPAD]
    og = pm[:, D_MPAD:]
    uc = _cat_rows(ucs) + cbias_ref[...]
    uc = uc * _sigmoid(uc)
    q = _mm(uc, wq_ref[...], NN, MODE_PROJ)
    k = _mm(uc, wk_ref[...], NN, MODE_PROJ)
    v = _mm(u, wv_ref[...], NN, MODE_PROJ)
    gp = (_mm(q, wgq_ref[...], NN, MODE_PROJ) + _mm(k, wgk_ref[...], NN, MODE_PROJ)
          + _mm(v, wgv_ref[...], NN, MODE_PROJ)) + bg_ref[...]
    k = k * (HD_M ** -0.5)

    ig = gp
    lf = -_softplus(-gp)
    if lv < L:
        valid = (lax.broadcasted_iota(jnp.int32, (GL, 1), 0) & (L - 1)) < lv
        ig = jnp.where(valid, ig, -jnp.inf)
        lf = jnp.where(valid, lf, 0.0)
    rg = lax.broadcasted_iota(jnp.int32, (GL, GL), 0)
    cg = lax.broadcasted_iota(jnp.int32, (GL, GL), 1)
    tri = jnp.logical_and(cg <= rg, (cg >> shift) == (rg >> shift))
    bcum = _mm(tri.astype(F32), lf, NN, "hi")
    lane = lax.broadcasted_iota(jnp.int32, (1, LANES), 1)
    gates = jnp.where(lane < H_M, ig, bcum)
    gates_t = gates.T
    head_lane = lane < HD_M
    ri = lax.broadcasted_iota(jnp.int32, (L, L), 0)
    ci = lax.broadcasted_iota(jnp.int32, (L, L), 1)
    causal = ci <= ri

    units = [(g, h) for g in seqs for h in range(H_M)]
    U = range(len(units))
    mm = functools.partial(_mm, mode=MODE_SCAN)
    rsl = [slice(g * L, (g + 1) * L) for g, _ in units]
    lsl = [slice(h * LANES, (h + 1) * LANES) for _, h in units]
    qh = [q[rsl[i], lsl[i]] for i in U]
    kh = [k[rsl[i], lsl[i]] for i in U]
    vh = [v[rsl[i], lsl[i]] for i in U]
    i_c = [gates[rsl[i], h:h + 1] for i, (g, h) in enumerate(units)]
    b_c = [gates[rsl[i], H_M + h:H_M + h + 1] for i, (g, h) in enumerate(units)]
    i_r = [gates_t[h:h + 1, rsl[i]] for i, (g, h) in enumerate(units)]
    b_r = [gates_t[H_M + h:H_M + h + 1, rsl[i]] for i, (g, h) in enumerate(units)]
    m_prev = [m_scr[g, h:h + 1, 0:1] for g, h in units]
    c_h = [c_scr[g, h] for g, h in units]
    n_h = [n_scr[g, h:h + 1, :] for g, h in units]
    qk = [mm(qh[i], kh[i], NT) for i in U]
    qc = [mm(qh[i], c_h[i], NT) for i in U]
    dm = [jnp.where(causal, b_c[i] - b_r[i] + i_r[i], -jnp.inf) for i in U]
    inter = [b_c[i] + m_prev[i] for i in U]
    m_t = [jnp.maximum(inter[i], jnp.max(dm[i], axis=-1, keepdims=True)) for i in U]
    s = [qk[i] * jnp.exp(dm[i] - m_t[i]) for i in U]
    w_i = [jnp.exp(inter[i] - m_t[i]) for i in U]
    num = [mm(s[i], vh[i], NN) + w_i[i] * qc[i] for i in U]
    den = [jnp.sum(s[i], axis=-1, keepdims=True) + w_i[i] * jnp.sum(qh[i] * n_h[i], axis=-1, keepdims=True)
           for i in U]
    hh = [num[i] / jnp.maximum(jnp.abs(den[i]), jnp.exp(-m_t[i])) for i in U]
    outs = []
    for i, (g, h) in enumerate(units):
        b_l = b_c[i][L - 1:L]
        m_new = m_t[i][L - 1:L]
        w_s = jnp.exp(b_l - b_c[i] + i_c[i] - m_new)
        w_p = jnp.exp(b_l + m_prev[i] - m_new)
        ks = w_s * kh[i]
        c_scr[g, h] = w_p * c_h[i] + mm(vh[i], ks, TN)
        n_scr[g, h:h + 1, :] = w_p * n_h[i] + jnp.sum(ks, axis=0, keepdims=True)
        m_scr[g, h:h + 1, :] = jnp.broadcast_to(m_new, (1, LANES))
        mean = jnp.sum(hh[i], axis=-1, keepdims=True) * (1.0 / HD_M)
        hc = jnp.where(head_lane, hh[i] - mean, 0.0)
        var = jnp.sum(hc * hc, axis=-1, keepdims=True) * (1.0 / HD_M)
        outs.append(hc * lax.rsqrt(var + GN_EPS_M))

    hn = _cat_rows([jnp.concatenate(outs[g * H_M:(g + 1) * H_M], axis=1) for g in seqs])
    y = (hn * gnw_ref[...] + skip_ref[...] * uc) * _sigmoid(og)
    for g in seqs:
        y_ref[g] = y[g * L:(g + 1) * L]

    @pl.when(c == pl.num_programs(1) - 1)
    def _():
        for g in seqs:
            cbo_ref[g] = xp_scr[g, 8 + lv - hist:8 + lv, :]
        co_ref[...] = c_scr[...]
        no_ref[...] = n_scr[...]
        mo_ref[...] = m_scr[...]

    for g in seqs:
        tail = xp_scr[g, 8 + L - hist:8 + L, :]
        xp_scr[g, 8 - hist:8, :] = tail


def _seq_spec(G, L, width):
    return pl.BlockSpec((G, L, width), lambda b, c: (b, c, 0))


def _state_spec(G, shape, l):
    zeros = (0,) * len(shape)
    return pl.BlockSpec((None, G) + tuple(shape), lambda b, c: (l, b) + zeros)


def _state_out_spec(G, shape):
    zeros = (0,) * len(shape)
    return pl.BlockSpec((G,) + tuple(shape), lambda b, c: (b,) + zeros)


def _param_spec(shape, l):
    zeros = (0,) * len(shape)
    return pl.BlockSpec((None,) + tuple(shape), lambda b, c: (l,) + zeros)


def _mixer_params():
    return pltpu.CompilerParams(dimension_semantics=("parallel", "arbitrary"), vmem_limit_bytes=VMEM_LIMIT)


def _chunking(B, T, t_valid, chunk, step_rows):
    L = min(T, chunk)
    nc = T // L
    lv = t_valid - (nc - 1) * L
    assert T % L == 0 and L & (L - 1) == 0 and 0 < lv <= L and (lv == L or nc == 1)
    G = max(1, min(B, step_rows // L, MAX_SEQS_PER_STEP))
    assert B % G == 0
    return L, nc, lv, G


def _conv_mixer(pc, st, W, l, t_valid):
    B, T, _ = pc.shape
    L, nc, lv, G = _chunking(B, T, t_valid, CONV_CHUNK, CONV_CHUNK)
    hist = CONV_W - 1
    return pl.pallas_call(
        functools.partial(_conv_body, G=G, L=L, lv=lv),
        grid=(B // G, nc),
        in_specs=[_seq_spec(G, L, 3 * D_CONV), _state_spec(G, (hist, D_CONV), l),
                  _param_spec((CONV_W, D_CONV), l)],
        out_specs=[_seq_spec(G, L, D_CONV), _state_out_spec(G, (hist, D_CONV))],
        out_shape=[jax.ShapeDtypeStruct((B, T, D_CONV), F32), jax.ShapeDtypeStruct((B, hist, D_CONV), F32)],
        scratch_shapes=[pltpu.VMEM((G, 8 + L, D_CONV), F32)],
        compiler_params=_mixer_params(),
        name="short_conv",
    )(pc, st["conv"], W["conv_w"])


def _rwkv_mixer(pr, st, W, l, t_valid):
    B, T, _ = pr.shape
    L, nc, lv, G = _chunking(B, T, t_valid, RWKV_CHUNK, RWKV_STEP_ROWS)
    vec = lambda n: _param_spec((1, n), l)
    return pl.pallas_call(
        functools.partial(_rwkv_body, G=G, L=L, lv=lv),
        grid=(B // G, nc),
        in_specs=[_seq_spec(G, L, P_RWKV), _state_spec(G, (1, P_RWKV), l),
                  _state_spec(G, (N_PAIR, LANES, LANES), l),
                  vec(P_RWKV), vec(D_RWKV), _param_spec((LANES, D_RWKV), l), vec(D_RWKV),
                  _param_spec((LANES, D_RWKV), l), _param_spec((LANES, D_RWKV), l),
                  vec(D_RWKV), vec(D_RWKV), vec(D_RWKV), vec(D_RWKV), vec(D_RWKV)],
        out_specs=[_seq_spec(G, L, D_RWKV), _state_out_spec(G, (1, P_RWKV)),
                   _state_out_spec(G, (N_PAIR, LANES, LANES))],
        out_shape=[jax.ShapeDtypeStruct((B, T, D_RWKV), F32), jax.ShapeDtypeStruct((B, 1, P_RWKV), F32),
                   jax.ShapeDtypeStruct((B, N_PAIR, LANES, LANES), F32)],
        scratch_shapes=[pltpu.VMEM((G, 8 + L, P_RWKV), F32), pltpu.VMEM((G, N_PAIR, LANES, LANES), F32)],
        compiler_params=_mixer_params(),
        name="rwkv7",
    )(pr, st["shift"], st["wkv"], W["rwkv_mu"], W["rwkv_w0"], W["rwkv_w_up"], W["rwkv_a0"], W["rwkv_a_up"],
      W["rwkv_g_up"], W["rwkv_k_k"], W["rwkv_k_a"], W["rwkv_r_k"], W["rwkv_ln_w"], W["rwkv_ln_b"])


def _mlstm_mixer(pm, st, W, l, t_valid):
    B, T, _ = pm.shape
    L, nc, lv, G = _chunking(B, T, t_valid, MLSTM_CHUNK, MLSTM_STEP_ROWS)
    hist = MCONV_W - 1
    vec = lambda n: _param_spec((1, n), l)
    sq = _param_spec((D_MPAD, D_MPAD), l)
    gate = _param_spec((D_MPAD, LANES), l)
    return pl.pallas_call(
        functools.partial(_mlstm_body, G=G, L=L, lv=lv),
        grid=(B // G, nc),
        in_specs=[_seq_spec(G, L, 2 * D_MPAD), _state_spec(G, (hist, D_MPAD), l),
                  _state_spec(G, (H_M, LANES, LANES), l), _state_spec(G, (H_M, LANES), l),
                  _state_spec(G, (H_M, LANES), l),
                  _param_spec((MCONV_W, D_MPAD), l), vec(D_MPAD), sq, sq, sq, gate, gate, gate, vec(LANES),
                  vec(D_MPAD), vec(D_MPAD)],
        out_specs=[_seq_spec(G, L, D_MPAD), _state_out_spec(G, (hist, D_MPAD)),
                   _state_out_spec(G, (H_M, LANES, LANES)), _state_out_spec(G, (H_M, LANES)),
                   _state_out_spec(G, (H_M, LANES))],
        out_shape=[jax.ShapeDtypeStruct((B, T, D_MPAD), F32), jax.ShapeDtypeStruct((B, hist, D_MPAD), F32),
                   jax.ShapeDtypeStruct((B, H_M, LANES, LANES), F32), jax.ShapeDtypeStruct((B, H_M, LANES), F32),
                   jax.ShapeDtypeStruct((B, H_M, LANES), F32)],
        scratch_shapes=[pltpu.VMEM((G, 8 + L, D_MPAD), F32), pltpu.VMEM((G, H_M, LANES, LANES), F32),
                        pltpu.VMEM((G, H_M, LANES), F32), pltpu.VMEM((G, H_M, LANES), F32)],
        compiler_params=_mixer_params(),
        name="mlstm",
    )(pm, st["mconv"], st["mC"], st["mn"], st["mm"], W["mlstm_conv_w"], W["mlstm_conv_b"], W["mlstm_wq"],
      W["mlstm_wk"], W["mlstm_wv"], W["mlstm_wgq"], W["mlstm_wgk"], W["mlstm_wgv"], W["mlstm_b_gate"],
      W["mlstm_gn_w"], W["mlstm_skip"])


def _pad_heads(x, axis):
    axis = axis % x.ndim
    shp = x.shape
    x = x.reshape(shp[:axis] + (H_M, HD_M) + shp[axis + 1:])
    pad = [(0, 0)] * x.ndim
    pad[axis + 1] = (0, LANES - HD_M)
    x = jnp.pad(x, pad)
    return x.reshape(shp[:axis] + (D_MPAD,) + shp[axis + 1:])


def _unpad_heads(x, axis):
    axis = axis % x.ndim
    shp = x.shape
    x = x.reshape(shp[:axis] + (H_M, LANES) + shp[axis + 1:])
    x = lax.slice_in_dim(x, 0, HD_M, axis=axis + 1)
    return x.reshape(shp[:axis] + (D_MLSTM,) + shp[axis + 1:])


def _head_block_diag(w):
    wp = jnp.pad(w, ((0, 0), (0, 0), (0, LANES - HD_M), (0, LANES - HD_M)))
    eye = jnp.eye(H_M, dtype=w.dtype)
    return jnp.einsum("lhde,hg->lhdge", wp, eye).reshape(w.shape[0], D_MPAD, D_MPAD)


def _prep_weights(norm_ff, ffn_gate, ffn_up, ffn_down, norm_mix, w_in, w_out, conv_w,
                  rwkv_mu, rwkv_w0, rwkv_w_up, rwkv_a0, rwkv_a_up, rwkv_g_up, rwkv_k_k, rwkv_k_a, rwkv_r_k,
                  rwkv_ln_w, rwkv_ln_b, mlstm_conv_w, mlstm_conv_b, mlstm_wq, mlstm_wk, mlstm_wv,
                  mlstm_w_gate, mlstm_b_gate, mlstm_gn_w, mlstm_skip, norm_final):
    depth = w_in.shape[0]
    pc, pr = 3 * D_CONV, P_RWKV
    row = lambda x: x[:, None, :]
    W = {}
    W["norm_ff"] = norm_ff[:, :, None, :]
    W["ffn_gate"] = ffn_gate.astype(BF16)
    W["ffn_up"] = ffn_up.astype(BF16)
    W["ffn_down"] = ffn_down.astype(BF16)
    W["norm_mix"] = row(norm_mix)
    W["w_in_c"] = w_in[:, :, :pc].astype(BF16)
    W["w_in_r"] = w_in[:, :, pc:pc + pr].astype(BF16)
    w_u = _pad_heads(w_in[:, :, pc + pr:pc + pr + D_MLSTM], 2)
    w_o = _pad_heads(w_in[:, :, pc + pr + D_MLSTM:], 2)
    W["w_in_m"] = jnp.concatenate([w_u, w_o], axis=2).astype(BF16)
    W["w_out_c"] = w_out[:, :D_CONV].astype(BF16)
    W["w_out_r"] = w_out[:, D_CONV:D_CONV + D_RWKV].astype(BF16)
    W["w_out_m"] = _pad_heads(w_out[:, D_CONV + D_RWKV:], 1).astype(BF16)
    W["conv_w"] = conv_w
    W["rwkv_mu"] = row(rwkv_mu)
    W["rwkv_w0"] = row(rwkv_w0)
    W["rwkv_a0"] = row(rwkv_a0)
    zl = lambda n: jnp.zeros((depth, n, D_RWKV), F32)
    W["rwkv_w_up"] = jnp.concatenate([rwkv_w_up, zl(LANES - LORA_W)], axis=1)
    W["rwkv_a_up"] = jnp.concatenate([zl(LORA_W), rwkv_a_up, zl(LORA_G)], axis=1)
    W["rwkv_g_up"] = jnp.concatenate([zl(LORA_W + LORA_A), rwkv_g_up], axis=1)
    W["rwkv_k_k"] = row(rwkv_k_k)
    W["rwkv_k_a"] = row(rwkv_k_a)
    W["rwkv_r_k"] = row(rwkv_r_k.reshape(depth, D_RWKV))
    W["rwkv_ln_w"] = row(rwkv_ln_w)
    W["rwkv_ln_b"] = row(rwkv_ln_b)
    W["mlstm_conv_w"] = _pad_heads(mlstm_conv_w, 2)
    W["mlstm_conv_b"] = row(_pad_heads(mlstm_conv_b, 1))
    W["mlstm_wq"] = _head_block_diag(mlstm_wq)
    W["mlstm_wk"] = _head_block_diag(mlstm_wk)
    W["mlstm_wv"] = _head_block_diag(mlstm_wv)
    wg = mlstm_w_gate.reshape(depth, H_M, 3, HD_M, 2 * H_M)
    for i, name in enumerate(("mlstm_wgq", "mlstm_wgk", "mlstm_wgv")):
        part = jnp.pad(wg[:, :, i], ((0, 0), (0, 0), (0, LANES - HD_M), (0, LANES - 2 * H_M)))
        W[name] = part.reshape(depth, D_MPAD, LANES)
    W["mlstm_b_gate"] = row(jnp.pad(mlstm_b_gate, ((0, 0), (0, LANES - 2 * H_M))))
    W["mlstm_gn_w"] = row(_pad_heads(mlstm_gn_w, 1))
    W["mlstm_skip"] = row(_pad_heads(mlstm_skip, 1))
    W["norm_final"] = norm_final[None, :]
    return W


def _wkv_to_pairs(s):
    d, b = s.shape[:2]
    s = s.reshape(d, b, N_PAIR, 2, HD_R, HD_R)
    eye = jnp.eye(2, dtype=s.dtype)
    return jnp.einsum("dbpivk,ij->dbpivjk", s, eye).reshape(d, b, N_PAIR, LANES, LANES)


def _wkv_from_pairs(s):
    d, b = s.shape[:2]
    s = s.reshape(d, b, N_PAIR, 2, HD_R, 2, HD_R)
    s = jnp.stack([s[:, :, :, 0, :, 0, :], s[:, :, :, 1, :, 1, :]], axis=3)
    return s.reshape(d, b, H_R, HD_R, HD_R)


def _prep_states(conv, shift, wkv, mconv, mC, mn, mm):
    pad_c = ((0, 0), (0, 0), (0, 0), (0, LANES - HD_M), (0, LANES - HD_M))
    return {
        "conv": conv,
        "shift": shift,
        "wkv": _wkv_to_pairs(wkv),
        "mconv": _pad_heads(mconv, 3),
        "mC": jnp.pad(mC, pad_c),
        "mn": jnp.pad(mn, ((0, 0), (0, 0), (0, 0), (0, LANES - HD_M))),
        "mm": jnp.broadcast_to(mm[..., None], mm.shape + (LANES,)),
    }


def _finish_states(outs):
    conv, shift, wkv, mconv, mC, mn, mm = (jnp.stack(o) for o in outs)
    return (conv, shift, _wkv_from_pairs(wkv), _unpad_heads(mconv, 3), mC[..., :HD_M, :HD_M],
            mn[..., :HD_M], mm[..., 0])


def _run_trunk(x, st, W, t_valid):
    B, T, D = x.shape
    depth = W["w_in_c"].shape[0]
    x2 = x.reshape(B * T, D)
    outs = [[] for _ in range(7)]
    seq = lambda a: a.reshape(B, T, a.shape[-1])
    for l in range(depth):
        x2 = _ffn(x2, W, l, 0)
        pc, pr, pm = _inproj(x2, W, l)
        yc, c_new = _conv_mixer(seq(pc), st, W, l, t_valid)
        yr, sh_new, s_new = _rwkv_mixer(seq(pr), st, W, l, t_valid)
        ym, mc_new, C_new, n_new, m_new = _mlstm_mixer(seq(pm), st, W, l, t_valid)
        flat = lambda a: a.reshape(B * T, a.shape[-1])
        x2 = _outproj(x2, flat(yc), flat(yr), flat(ym), W, l)
        x2 = _ffn(x2, W, l, 1)
        for lst, s in zip(outs, (c_new, sh_new, s_new, mc_new, C_new, n_new, m_new)):
            lst.append(s)
    y = _final_norm(x2, W["norm_final"]).reshape(B, T, D)
    return y, _finish_states(outs)


def kernel(x_prompt, x_sample, cache_conv, cache_shift, state_wkv, cache_mconv, state_mC, state_mn, state_mm, norm_ff, ffn_gate, ffn_up, ffn_down, norm_mix, w_in, w_out, conv_w, rwkv_mu, rwkv_w0, rwkv_w_up, rwkv_a0, rwkv_a_up, rwkv_g_up, rwkv_k_k, rwkv_k_a, rwkv_r_k, rwkv_ln_w, rwkv_ln_b, mlstm_conv_w, mlstm_conv_b, mlstm_wq, mlstm_wk, mlstm_wv, mlstm_w_gate, mlstm_b_gate, mlstm_gn_w, mlstm_skip, norm_final):
    W = _prep_weights(norm_ff, ffn_gate, ffn_up, ffn_down, norm_mix, w_in, w_out, conv_w,
                      rwkv_mu, rwkv_w0, rwkv_w_up, rwkv_a0, rwkv_a_up, rwkv_g_up, rwkv_k_k, rwkv_k_a, rwkv_r_k,
                      rwkv_ln_w, rwkv_ln_b, mlstm_conv_w, mlstm_conv_b, mlstm_wq, mlstm_wk, mlstm_wv,
                      mlstm_w_gate, mlstm_b_gate, mlstm_gn_w, mlstm_skip, norm_final)
    depth = w_in.shape[0]
    Bp, Tp, _ = x_prompt.shape
    Bs, Ts, _ = x_sample.shape
    zeros = lambda *shape: jnp.zeros((depth, Bp) + shape, F32)
    st_p = _prep_states(zeros(CONV_W - 1, D_CONV), zeros(1, P_RWKV), zeros(H_R, HD_R, HD_R),
                        zeros(MCONV_W - 1, D_MLSTM), zeros(H_M, HD_M, HD_M), zeros(H_M, HD_M), zeros(H_M))
    y_prompt, p_states = _run_trunk(x_prompt, st_p, W, Tp)

    st_s = _prep_states(cache_conv, cache_shift, state_wkv, cache_mconv, state_mC, state_mn, state_mm)
    xs = jnp.pad(x_sample, ((0, 0), (0, SAMPLE_PAD_T - Ts), (0, 0)))
    y_sample, s_states = _run_trunk(xs, st_s, W, Ts)
    y_sample = y_sample[:, :Ts]
    return (y_prompt, y_sample) + tuple(p_states) + tuple(s_states)
```

```python
import functools

import jax
import jax.numpy as jnp
from jax import lax
from jax.experimental import pallas as pl
from jax.experimental.pallas import tpu as pltpu

F32 = jnp.float32
BF16 = jnp.bfloat16
HI = lax.Precision.HIGHEST

LANES = 128
H_R, HD_R = 6, 64
D_RWKV = H_R * HD_R
N_PAIR = H_R // 2
LORA_W, LORA_A, LORA_G = 32, 32, 64
P_RWKV = 3 * D_RWKV + LORA_W + LORA_A + LORA_G
H_M, HD_M = 4, 96
D_MLSTM = H_M * HD_M
D_MPAD = H_M * LANES
D_CONV = 256
CONV_W, MCONV_W = 3, 4
RMS_EPS = 1e-6
GN_EPS_R = 64e-5
GN_EPS_M = 1e-6
SAMPLE_PAD_T = 8
RWKV_CHUNK = 64
RWKV_STEP_ROWS = 256
MLSTM_CHUNK = 256
MLSTM_STEP_ROWS = 256
CONV_CHUNK = 512
MAX_SEQS_PER_STEP = 8
TOKEN_TILE = 512
FFN_COLS = 256
VMEM_LIMIT = 48 * 1024 * 1024

NN = ((1,), (0,))
NT = ((1,), (1,))
TN = ((0,), (0,))

MODE_PROJ = "b1"
MODE_SCAN = "b1"


def _mm(a, b, dims=NN, mode="b1"):
    dn = (dims, ((), ()))
    if mode == "hi":
        return lax.dot_general(a, b, dn, precision=HI, preferred_element_type=F32)
    dot = lambda x, y: lax.dot_general(x, y, dn, preferred_element_type=F32)
    ah = a.astype(BF16)
    bh = b.astype(BF16)
    if mode == "b1":
        return dot(ah, bh)
    al = (a - ah.astype(F32)).astype(BF16)
    bl = (b - bh.astype(F32)).astype(BF16)
    return dot(ah, bh) + (dot(ah, bl) + dot(al, bh))


def _rms(x, g):
    return x * lax.rsqrt(jnp.mean(x * x, axis=-1, keepdims=True) + RMS_EPS) * g


def _sigmoid(x):
    return 1.0 / (1.0 + jnp.exp(-x))


def _softplus(x):
    return jnp.maximum(x, 0.0) + jnp.log1p(jnp.exp(-jnp.abs(x)))


def _cat_rows(xs):
    return xs[0] if len(xs) == 1 else jnp.concatenate(xs, axis=0)


def _half_ffn(x, g_ref, wg_ref, wu_ref, wd_ref):
    n = _rms(x, g_ref[...]).astype(BF16)
    acc = jnp.zeros(x.shape, F32)
    for c in range(wg_ref.shape[1] // FFN_COLS):
        sl = slice(c * FFN_COLS, (c + 1) * FFN_COLS)
        hg = jnp.dot(n, wg_ref[:, sl], preferred_element_type=F32)
        hu = jnp.dot(n, wu_ref[:, sl], preferred_element_type=F32)
        h = (hg * _sigmoid(hg) * hu).astype(BF16)
        acc = acc + jnp.dot(h, wd_ref[sl, :], preferred_element_type=F32)
    return x + 0.5 * acc


def _ffn_body(x_ref, g_ref, wg_ref, wu_ref, wd_ref, o_ref):
    o_ref[...] = _half_ffn(x_ref[...], g_ref, wg_ref, wu_ref, wd_ref)


def _mix_ffn_body(x_ref, yc_ref, yr_ref, ym_ref, wc_ref, wr_ref, wm_ref, g_ref, wg_ref, wu_ref, wd_ref,
                  gf_ref, o_ref, *, final):
    acc = jnp.dot(yc_ref[...].astype(BF16), wc_ref[...], preferred_element_type=F32)
    acc = acc + jnp.dot(yr_ref[...].astype(BF16), wr_ref[...], preferred_element_type=F32)
    acc = acc + jnp.dot(ym_ref[...].astype(BF16), wm_ref[...], preferred_element_type=F32)
    x = _half_ffn(x_ref[...] + acc, g_ref, wg_ref, wu_ref, wd_ref)
    o_ref[...] = _rms(x, gf_ref[...]) if final else x


def _inproj_body(x_ref, g_ref, wc_ref, wr_ref, wm_ref, pc_ref, pr_ref, pm_ref):
    n = _rms(x_ref[...], g_ref[...]).astype(BF16)
    pc_ref[...] = jnp.dot(n, wc_ref[...], preferred_element_type=F32)
    pr_ref[...] = jnp.dot(n, wr_ref[...], preferred_element_type=F32)
    pm_ref[...] = jnp.dot(n, wm_ref[...], preferred_element_type=F32)


def _rows(tm, width):
    return pl.BlockSpec((tm, width), lambda i: (i, 0))


def _resident(shape, *lead):
    nlead = len(lead)
    zeros = (0,) * len(shape)
    return pl.BlockSpec((None,) * nlead + tuple(shape), lambda i: tuple(lead) + zeros,
                        pipeline_mode=pl.Buffered(1))


def _token_params():
    return pltpu.CompilerParams(dimension_semantics=("parallel",), vmem_limit_bytes=VMEM_LIMIT)


def _ffn(x2, W, l, which):
    n, d = x2.shape
    f = W["ffn_gate"].shape[-1]
    tm = min(TOKEN_TILE, n)
    return pl.pallas_call(
        _ffn_body,
        grid=(n // tm,),
        in_specs=[_rows(tm, d), _resident((1, d), l, which), _resident((d, f), l, which),
                  _resident((d, f), l, which), _resident((f, d), l, which)],
        out_specs=_rows(tm, d),
        out_shape=jax.ShapeDtypeStruct((n, d), F32),
        compiler_params=_token_params(),
        name="ffn",
    )(x2, W["norm_ff"], W["ffn_gate"], W["ffn_up"], W["ffn_down"])


def _inproj(x2, W, l):
    n, d = x2.shape
    tm = min(TOKEN_TILE, n)
    wc, wr, wm = W["w_in_c"], W["w_in_r"], W["w_in_m"]
    widths = (wc.shape[-1], wr.shape[-1], wm.shape[-1])
    return pl.pallas_call(
        _inproj_body,
        grid=(n // tm,),
        in_specs=[_rows(tm, d), _resident((1, d), l)] + [_resident((d, w), l) for w in widths],
        out_specs=[_rows(tm, w) for w in widths],
        out_shape=[jax.ShapeDtypeStruct((n, w), F32) for w in widths],
        compiler_params=_token_params(),
        name="inproj",
    )(x2, W["norm_mix"], wc, wr, wm)


def _mix_ffn(x2, yc, yr, ym, W, l, final):
    n, d = x2.shape
    f = W["ffn_gate"].shape[-1]
    tm = min(TOKEN_TILE, n)
    ws = (W["w_out_c"], W["w_out_r"], W["w_out_m"])
    return pl.pallas_call(
        functools.partial(_mix_ffn_body, final=final),
        grid=(n // tm,),
        in_specs=[_rows(tm, d)] + [_rows(tm, w.shape[1]) for w in ws] + [_resident(w.shape[1:], l) for w in ws]
                 + [_resident((1, d), l, 1), _resident((d, f), l, 1), _resident((d, f), l, 1),
                    _resident((f, d), l, 1), _resident((1, d))],
        out_specs=_rows(tm, d),
        out_shape=jax.ShapeDtypeStruct((n, d), F32),
        compiler_params=_token_params(),
        name="mix_ffn",
    )(x2, yc, yr, ym, *ws, W["norm_ff"], W["ffn_gate"], W["ffn_up"], W["ffn_down"], W["norm_final"])


def _conv_body(pc_ref, buf_ref, w_ref, nb_in, y_ref, nb_ref, xp_scr, *, G, L, lv):
    c = pl.program_id(1)
    hist = CONV_W - 1
    w = w_ref[...]
    for g in range(G):
        @pl.when(c == 0)
        def _():
            xp_scr[g, 8 - hist:8, :] = buf_ref[g]

        p = pc_ref[g]
        b_gate = p[:, :D_CONV]
        xp_scr[g, 8:8 + L, :] = p[:, D_CONV:2 * D_CONV] * p[:, 2 * D_CONV:]
        z = w[0:1] * xp_scr[g, 6:6 + L, :]
        z = z + w[1:2] * xp_scr[g, 7:7 + L, :]
        z = z + w[2:3] * xp_scr[g, 8:8 + L, :]
        y_ref[g] = b_gate * z

        @pl.when(c == pl.num_programs(1) - 1)
        def _():
            nb_ref[g] = xp_scr[g, 8 + lv - hist:8 + lv, :]

        tail = xp_scr[g, 8 + L - hist:8 + L, :]
        xp_scr[g, 8 - hist:8, :] = tail


def _rwkv_body(pr_ref, sh_ref, s0_ref, mu_ref, w0_ref, wup_ref, a0_ref, aup_ref, gup_ref, kk_ref, ka_ref,
               rk_ref, lnw_ref, lnb_ref, sho_in, so_in, y_ref, sho_ref, so_ref, xp_scr, s_scr, *, G, L, lv):
    c = pl.program_id(1)
    GL = G * L
    shift = L.bit_length() - 1
    seqs = range(G)

    @pl.when(c == 0)
    def _():
        for g in seqs:
            xp_scr[g, 7:8, :] = sh_ref[g]
            for j in range(N_PAIR):
                s_scr[g, j] = jnp.zeros((LANES, LANES), F32)
                s_scr[g, j, :HD_R, :HD_R] = s0_ref[g, 2 * j]
                s_scr[g, j, HD_R:, HD_R:] = s0_ref[g, 2 * j + 1]

    for g in seqs:
        xp_scr[g, 8:8 + L, :] = pr_ref[g]
    p = _cat_rows([pr_ref[g] for g in seqs])
    prev = _cat_rows([xp_scr[g, 7:7 + L, :] for g in seqs])
    ps = p + (prev - p) * mu_ref[...]
    r = ps[:, 0:D_RWKV]
    k = ps[:, D_RWKV:2 * D_RWKV]
    v = ps[:, 2 * D_RWKV:3 * D_RWKV]
    x4 = ps[:, 3 * D_RWKV:]
    w_log = -_softplus(-(w0_ref[...] + _mm(jnp.tanh(x4), wup_ref[...], NN, MODE_PROJ))) - 0.5
    lw = -jnp.exp(w_log)
    a = _sigmoid(a0_ref[...] + _mm(x4, aup_ref[...], NN, MODE_PROJ))
    gate = _mm(_sigmoid(x4), gup_ref[...], NN, MODE_PROJ)

    lane = lax.broadcasted_iota(jnp.int32, (1, LANES), 1)
    lo = lane < HD_R

    def pair_sum(x):
        s0 = jnp.sum(jnp.where(lo, x, 0.0), axis=-1, keepdims=True)
        s1 = jnp.sum(jnp.where(lo, 0.0, x), axis=-1, keepdims=True)
        return jnp.where(lo, s0, s1)

    def head_sum(x):
        return jnp.concatenate([pair_sum(x[:, j * LANES:(j + 1) * LANES]) for j in range(N_PAIR)], axis=1)

    kk = k * kk_ref[...]
    kk = kk * lax.rsqrt(jnp.maximum(head_sum(kk * kk), 1e-24))
    k2 = k * (1.0 + (a - 1.0) * ka_ref[...])
    bonus = head_sum(r * k2 * rk_ref[...]) * v

    if lv < L:
        valid = (lax.broadcasted_iota(jnp.int32, (GL, 1), 0) & (L - 1)) < lv
        lw = jnp.where(valid, lw, 0.0)
        kk = jnp.where(valid, kk, 0.0)
        k2 = jnp.where(valid, k2, 0.0)

    ri = lax.broadcasted_iota(jnp.int32, (GL, GL), 0)
    ci = lax.broadcasted_iota(jnp.int32, (GL, GL), 1)
    tri = jnp.logical_and(ci <= ri, (ci >> shift) == (ri >> shift))
    cl = _mm(tri.astype(F32), lw, NN, "hi")
    e_pos = jnp.exp(cl)
    e_neg = jnp.exp(-cl)
    a_t = -kk * jnp.exp(cl - lw)
    b_t = kk * a * e_neg
    k_t = k2 * e_neg
    r_t = r * e_pos

    L2 = 2 * L
    ri2 = lax.broadcasted_iota(jnp.int32, (L2, L2), 0)
    ci2 = lax.broadcasted_iota(jnp.int32, (L2, L2), 1)
    tri2 = ri2 & (L - 1)
    tci2 = ci2 & (L - 1)
    strict = tci2 < tri2
    incl = tci2 <= tri2
    eye = (ri2 == ci2).astype(F32)
    row_lo = lax.broadcasted_iota(jnp.int32, (L2, 1), 0) < L
    own = jnp.logical_and(row_lo, lo) | jnp.logical_and(jnp.logical_not(row_lo), jnp.logical_not(lo))

    def stack(x):
        return jnp.concatenate([jnp.where(lo, x, 0.0), jnp.where(lo, 0.0, x)], axis=0)

    units = [(g, j) for g in seqs for j in range(N_PAIR)]
    U = range(len(units))
    rsl = [slice(g * L, (g + 1) * L) for g, _ in units]
    lsl = [slice(j * LANES, (j + 1) * LANES) for _, j in units]
    mm = functools.partial(_mm, mode=MODE_SCAN)
    xa, xb, xk, xr, xv = ([stack(t[rsl[i], lsl[i]]) for i in U] for t in (a_t, b_t, k_t, r_t, v))
    s_bd = [s_scr[g, j] for g, j in units]
    xar = [jnp.concatenate([xa[i], xr[i]], axis=0) for i in U]
    gram = [mm(xar[i], jnp.concatenate([xb[i], xk[i]], axis=0), NT) for i in U]
    a_ab = [jnp.where(strict, gm[:L2, :L2], 0.0) for gm in gram]
    a_ak = [jnp.where(strict, gm[:L2, L2:], 0.0) for gm in gram]
    m_rb = [jnp.where(incl, gm[L2:, :L2], 0.0) for gm in gram]
    m_rk = [jnp.where(incl, gm[L2:, L2:], 0.0) for gm in gram]
    sh = [mm(xar[i], s_bd[i], NT) for i in U]
    akv = [sh[i] + mm(jnp.concatenate([a_ak[i], m_rk[i]], axis=0), xv[i], NN) for i in U]
    tinv = [eye + m for m in a_ab]
    steps = L.bit_length() - 2
    if steps > 0:
        pw = [mm(m, m, NN) for m in a_ab]
    for step in range(steps):
        if step < steps - 1:
            both = [mm(pw[i], jnp.concatenate([tinv[i], pw[i]], axis=1), NN) for i in U]
            tinv = [tinv[i] + both[i][:, :L2] for i in U]
            pw = [both[i][:, L2:] for i in U]
        else:
            tinv = [tinv[i] + mm(pw[i], tinv[i], NN) for i in U]
    u = [mm(tinv[i], akv[i][:L2], NN) for i in U]
    o = [akv[i][L2:] + mm(m_rb[i], u[i], NN) for i in U]
    for i, (g, j) in enumerate(units):
        g_last = e_pos[g * L + L - 1:g * L + L, lsl[i]]
        upd = mm(jnp.concatenate([u[i], xv[i]], axis=0), jnp.concatenate([xb[i], xk[i]], axis=0), TN)
        s_scr[g, j] = (s_bd[i] + upd) * g_last
    outs = []
    for i in U:
        mean = jnp.sum(o[i], axis=-1, keepdims=True) * (1.0 / HD_R)
        oc = jnp.where(own, o[i] - mean, 0.0)
        var = jnp.sum(oc * oc, axis=-1, keepdims=True) * (1.0 / HD_R)
        on = oc * lax.rsqrt(var + GN_EPS_R)
        outs.append(on[:L] + on[L:])

    yn = _cat_rows([jnp.concatenate(outs[g * N_PAIR:(g + 1) * N_PAIR], axis=1) for g in seqs])
    y = (yn * lnw_ref[...] + lnb_ref[...] + bonus) * gate
    for g in seqs:
        y_ref[g] = y[g * L:(g + 1) * L]

    @pl.when(c == pl.num_programs(1) - 1)
    def _():
        for g in seqs:
            sho_ref[g] = xp_scr[g, 8 + lv - 1:8 + lv, :]
            for j in range(N_PAIR):
                so_ref[g, 2 * j] = s_scr[g, j, :HD_R, :HD_R]
                so_ref[g, 2 * j + 1] = s_scr[g, j, HD_R:, HD_R:]

    for g in seqs:
        xp_scr[g, 7:8, :] = xp_scr[g, 7 + L:8 + L, :]


def _mlstm_body(pm_ref, cb_ref, c0_ref, n0_ref, m0_ref, cw_ref, cbias_ref, wq_ref, wk_ref, wv_ref,
                wgq_ref, wgk_ref, wgv_ref, bg_ref, gnw_ref, skip_ref,
                cbo_in, co_in, no_in, mo_in,
                y_ref, cbo_ref, co_ref, no_ref, mo_ref, xp_scr, c_scr, n_scr, m_scr, *, G, L, lv):
    c = pl.program_id(1)
    hist = MCONV_W - 1
    GL = G * L
    shift = L.bit_length() - 1
    seqs = range(G)

    @pl.when(c == 0)
    def _():
        for g in seqs:
            xp_scr[g, 0:8, :] = jnp.zeros((8, D_MPAD), F32)
            n_scr[g] = jnp.zeros((H_M, LANES), F32)
            n_scr[g, :, :HD_M] = n0_ref[g]
            cb = cb_ref[g]
            m0 = m0_ref[g]
            for h in range(H_M):
                xp_scr[g, 8 - hist:8, h * LANES:h * LANES + HD_M] = cb[:, h * HD_M:(h + 1) * HD_M]
                c_scr[g, h] = jnp.zeros((LANES, LANES), F32)
                c_scr[g, h, :HD_M, :HD_M] = c0_ref[g, h]
                m_scr[g, h:h + 1, :] = jnp.broadcast_to(m0[:, h:h + 1], (1, LANES))

    cw = cw_ref[...]
    ucs = []
    for g in seqs:
        u_g = pm_ref[g][:, :D_MPAD]
        xp_scr[g, 8:8 + L, :] = u_g
        acc = cw[0:1] * xp_scr[g, 5:5 + L, :]
        acc = acc + cw[1:2] * xp_scr[g, 6:6 + L, :]
        acc = acc + cw[2:3] * xp_scr[g, 7:7 + L, :]
        ucs.append(acc + cw[3:4] * u_g)
    pm = _cat_rows([pm_ref[g] for g in seqs])
    u = pm[:, :D_MPAD]
    og = pm[:, D_MPAD:]
    uc = _cat_rows(ucs) + cbias_ref[...]
    uc = uc * _sigmoid(uc)
    q = _mm(uc, wq_ref[...], NN, MODE_PROJ)
    k = _mm(uc, wk_ref[...], NN, MODE_PROJ)
    v = _mm(u, wv_ref[...], NN, MODE_PROJ)
    gp = (_mm(q, wgq_ref[...], NN, MODE_PROJ) + _mm(k, wgk_ref[...], NN, MODE_PROJ)
          + _mm(v, wgv_ref[...], NN, MODE_PROJ)) + bg_ref[...]
    k = k * (HD_M ** -0.5)

    ig = gp
    lf = -_softplus(-gp)
    if lv < L:
        valid = (lax.broadcasted_iota(jnp.int32, (GL, 1), 0) & (L - 1)) < lv
        ig = jnp.where(valid, ig, -jnp.inf)
        lf = jnp.where(valid, lf, 0.0)
    rg = lax.broadcasted_iota(jnp.int32, (GL, GL), 0)
    cg = lax.broadcasted_iota(jnp.int32, (GL, GL), 1)
    tri = jnp.logical_and(cg <= rg, (cg >> shift) == (rg >> shift))
    bcum = _mm(tri.astype(F32), lf, NN, "hi")
    lane = lax.broadcasted_iota(jnp.int32, (1, LANES), 1)
    gates = jnp.where(lane < H_M, ig, bcum)
    gates_t = gates.T
    head_lane = lane < HD_M
    ri = lax.broadcasted_iota(jnp.int32, (L, L), 0)
    ci = lax.broadcasted_iota(jnp.int32, (L, L), 1)
    causal = ci <= ri

    units = [(g, h) for g in seqs for h in range(H_M)]
    U = range(len(units))
    mm = functools.partial(_mm, mode=MODE_SCAN)
    rsl = [slice(g * L, (g + 1) * L) for g, _ in units]
    lsl = [slice(h * LANES, (h + 1) * LANES) for _, h in units]
    qh = [q[rsl[i], lsl[i]] for i in U]
    kh = [k[rsl[i], lsl[i]] for i in U]
    vh = [v[rsl[i], lsl[i]] for i in U]
    i_c = [gates[rsl[i], h:h + 1] for i, (g, h) in enumerate(units)]
    b_c = [gates[rsl[i], H_M + h:H_M + h + 1] for i, (g, h) in enumerate(units)]
    i_r = [gates_t[h:h + 1, rsl[i]] for i, (g, h) in enumerate(units)]
    b_r = [gates_t[H_M + h:H_M + h + 1, rsl[i]] for i, (g, h) in enumerate(units)]
    m_prev = [m_scr[g, h:h + 1, 0:1] for g, h in units]
    c_h = [c_scr[g, h] for g, h in units]
    n_h = [n_scr[g, h:h + 1, :] for g, h in units]
    qk = [mm(qh[i], kh[i], NT) for i in U]
    qc = [mm(qh[i], c_h[i], NT) for i in U]
    dm = [jnp.where(causal, b_c[i] - b_r[i] + i_r[i], -jnp.inf) for i in U]
    inter = [b_c[i] + m_prev[i] for i in U]
    m_t = [jnp.maximum(inter[i], jnp.max(dm[i], axis=-1, keepdims=True)) for i in U]
    s = [qk[i] * jnp.exp(dm[i] - m_t[i]) for i in U]
    w_i = [jnp.exp(inter[i] - m_t[i]) for i in U]
    num = [mm(s[i], vh[i], NN) + w_i[i] * qc[i] for i in U]
    den = [jnp.sum(s[i], axis=-1, keepdims=True) + w_i[i] * jnp.sum(qh[i] * n_h[i], axis=-1, keepdims=True)
           for i in U]
    hh = [num[i] / jnp.maximum(jnp.abs(den[i]), jnp.exp(-m_t[i])) for i in U]
    outs = []
    for i, (g, h) in enumerate(units):
        b_l = b_c[i][L - 1:L]
        m_new = m_t[i][L - 1:L]
        w_s = jnp.exp(b_l - b_c[i] + i_c[i] - m_new)
        w_p = jnp.exp(b_l + m_prev[i] - m_new)
        ks = w_s * kh[i]
        c_scr[g, h] = w_p * c_h[i] + mm(vh[i], ks, TN)
        n_scr[g, h:h + 1, :] = w_p * n_h[i] + jnp.sum(ks, axis=0, keepdims=True)
        m_scr[g, h:h + 1, :] = jnp.broadcast_to(m_new, (1, LANES))
        mean = jnp.sum(hh[i], axis=-1, keepdims=True) * (1.0 / HD_M)
        hc = jnp.where(head_lane, hh[i] - mean, 0.0)
        var = jnp.sum(hc * hc, axis=-1, keepdims=True) * (1.0 / HD_M)
        outs.append(hc * lax.rsqrt(var + GN_EPS_M))

    hn = _cat_rows([jnp.concatenate(outs[g * H_M:(g + 1) * H_M], axis=1) for g in seqs])
    y = (hn * gnw_ref[...] + skip_ref[...] * uc) * _sigmoid(og)
    for g in seqs:
        y_ref[g] = y[g * L:(g + 1) * L]

    @pl.when(c == pl.num_programs(1) - 1)
    def _():
        lane_h = lax.broadcasted_iota(jnp.int32, (1, H_M), 1)
        for g in seqs:
            rows = xp_scr[g, 8 + lv - hist:8 + lv, :]
            cbo_ref[g] = jnp.concatenate([rows[:, h * LANES:h * LANES + HD_M] for h in range(H_M)], axis=1)
            no_ref[g] = n_scr[g][:, :HD_M]
            m_row = jnp.zeros((1, H_M), F32)
            for h in range(H_M):
                co_ref[g, h] = c_scr[g, h, :HD_M, :HD_M]
                m_row = jnp.where(lane_h == h, m_scr[g, h:h + 1, 0:1], m_row)
            mo_ref[g] = m_row

    for g in seqs:
        tail = xp_scr[g, 8 + L - hist:8 + L, :]
        xp_scr[g, 8 - hist:8, :] = tail


def _seq_spec(G, L, width):
    return pl.BlockSpec((G, L, width), lambda b, c: (b, c, 0))


def _state_spec(G, shape, l):
    zeros = (0,) * len(shape)
    return pl.BlockSpec((None, G) + tuple(shape), lambda b, c: (l, b) + zeros)


def _param_spec(shape, l):
    zeros = (0,) * len(shape)
    return pl.BlockSpec((None,) + tuple(shape), lambda b, c: (l,) + zeros)


_IN_PLACE = pl.BlockSpec(memory_space=pl.ANY)


def _mixer_params():
    return pltpu.CompilerParams(dimension_semantics=("parallel", "arbitrary"), vmem_limit_bytes=VMEM_LIMIT)


def _chunking(B, T, t_valid, chunk, step_rows):
    L = min(T, chunk)
    nc = T // L
    lv = t_valid - (nc - 1) * L
    assert T % L == 0 and L & (L - 1) == 0 and 0 < lv <= L and (lv == L or nc == 1)
    G = max(1, min(B, step_rows // L, MAX_SEQS_PER_STEP))
    assert B % G == 0
    return L, nc, lv, G


def _conv_mixer(pc, st, out, W, l, t_valid):
    B, T, _ = pc.shape
    L, nc, lv, G = _chunking(B, T, t_valid, CONV_CHUNK, CONV_CHUNK)
    hist = CONV_W - 1
    state = _state_spec(G, (hist, D_CONV), l)
    return pl.pallas_call(
        functools.partial(_conv_body, G=G, L=L, lv=lv),
        grid=(B // G, nc),
        in_specs=[_seq_spec(G, L, 3 * D_CONV), state, _param_spec((CONV_W, D_CONV), l), _IN_PLACE],
        out_specs=[_seq_spec(G, L, D_CONV), state],
        out_shape=[jax.ShapeDtypeStruct((B, T, D_CONV), F32), jax.ShapeDtypeStruct(out["conv"].shape, F32)],
        input_output_aliases={3: 1},
        scratch_shapes=[pltpu.VMEM((G, 8 + L, D_CONV), F32)],
        compiler_params=_mixer_params(),
        name="short_conv",
    )(pc, st["conv"], W["conv_w"], out["conv"])


def _rwkv_mixer(pr, st, out, W, l, t_valid):
    B, T, _ = pr.shape
    L, nc, lv, G = _chunking(B, T, t_valid, RWKV_CHUNK, RWKV_STEP_ROWS)
    vec = lambda n: _param_spec((1, n), l)
    shift = _state_spec(G, (1, P_RWKV), l)
    wkv = _state_spec(G, (H_R, HD_R, HD_R), l)
    return pl.pallas_call(
        functools.partial(_rwkv_body, G=G, L=L, lv=lv),
        grid=(B // G, nc),
        in_specs=[_seq_spec(G, L, P_RWKV), shift, wkv,
                  vec(P_RWKV), vec(D_RWKV), _param_spec((LANES, D_RWKV), l), vec(D_RWKV),
                  _param_spec((LANES, D_RWKV), l), _param_spec((LANES, D_RWKV), l),
                  vec(D_RWKV), vec(D_RWKV), vec(D_RWKV), vec(D_RWKV), vec(D_RWKV), _IN_PLACE, _IN_PLACE],
        out_specs=[_seq_spec(G, L, D_RWKV), shift, wkv],
        out_shape=[jax.ShapeDtypeStruct((B, T, D_RWKV), F32), jax.ShapeDtypeStruct(out["shift"].shape, F32),
                   jax.ShapeDtypeStruct(out["wkv"].shape, F32)],
        input_output_aliases={14: 1, 15: 2},
        scratch_shapes=[pltpu.VMEM((G, 8 + L, P_RWKV), F32), pltpu.VMEM((G, N_PAIR, LANES, LANES), F32)],
        compiler_params=_mixer_params(),
        name="rwkv7",
    )(pr, st["shift"], st["wkv"], W["rwkv_mu"], W["rwkv_w0"], W["rwkv_w_up"], W["rwkv_a0"], W["rwkv_a_up"],
      W["rwkv_g_up"], W["rwkv_k_k"], W["rwkv_k_a"], W["rwkv_r_k"], W["rwkv_ln_w"], W["rwkv_ln_b"],
      out["shift"], out["wkv"])


def _mlstm_mixer(pm, st, out, W, l, t_valid):
    B, T, _ = pm.shape
    L, nc, lv, G = _chunking(B, T, t_valid, MLSTM_CHUNK, MLSTM_STEP_ROWS)
    hist = MCONV_W - 1
    vec = lambda n: _param_spec((1, n), l)
    sq = _param_spec((D_MPAD, D_MPAD), l)
    gate = _param_spec((D_MPAD, LANES), l)
    states = [_state_spec(G, (hist, D_MLSTM), l), _state_spec(G, (H_M, HD_M, HD_M), l),
              _state_spec(G, (H_M, HD_M), l), _state_spec(G, (1, H_M), l)]
    names = ("mconv", "mC", "mn", "mm")
    return pl.pallas_call(
        functools.partial(_mlstm_body, G=G, L=L, lv=lv),
        grid=(B // G, nc),
        in_specs=[_seq_spec(G, L, 2 * D_MPAD)] + states
                 + [_param_spec((MCONV_W, D_MPAD), l), vec(D_MPAD), sq, sq, sq, gate, gate, gate, vec(LANES),
                    vec(D_MPAD), vec(D_MPAD)] + [_IN_PLACE] * 4,
        out_specs=[_seq_spec(G, L, D_MPAD)] + states,
        out_shape=[jax.ShapeDtypeStruct((B, T, D_MPAD), F32)]
                  + [jax.ShapeDtypeStruct(out[n].shape, F32) for n in names],
        input_output_aliases={16: 1, 17: 2, 18: 3, 19: 4},
        scratch_shapes=[pltpu.VMEM((G, 8 + L, D_MPAD), F32), pltpu.VMEM((G, H_M, LANES, LANES), F32),
                        pltpu.VMEM((G, H_M, LANES), F32), pltpu.VMEM((G, H_M, LANES), F32)],
        compiler_params=_mixer_params(),
        name="mlstm",
    )(pm, st["mconv"], st["mC"], st["mn"], st["mm"], W["mlstm_conv_w"], W["mlstm_conv_b"], W["mlstm_wq"],
      W["mlstm_wk"], W["mlstm_wv"], W["mlstm_wgq"], W["mlstm_wgk"], W["mlstm_wgv"], W["mlstm_b_gate"],
      W["mlstm_gn_w"], W["mlstm_skip"], *[out[n] for n in names])


def _pad_heads(x, axis):
    axis = axis % x.ndim
    shp = x.shape
    x = x.reshape(shp[:axis] + (H_M, HD_M) + shp[axis + 1:])
    pad = [(0, 0)] * x.ndim
    pad[axis + 1] = (0, LANES - HD_M)
    x = jnp.pad(x, pad)
    return x.reshape(shp[:axis] + (D_MPAD,) + shp[axis + 1:])


def _head_block_diag(w):
    wp = jnp.pad(w, ((0, 0), (0, 0), (0, LANES - HD_M), (0, LANES - HD_M)))
    eye = jnp.eye(H_M, dtype=w.dtype)
    return jnp.einsum("lhde,hg->lhdge", wp, eye).reshape(w.shape[0], D_MPAD, D_MPAD)


def _prep_weights(norm_ff, ffn_gate, ffn_up, ffn_down, norm_mix, w_in, w_out, conv_w,
                  rwkv_mu, rwkv_w0, rwkv_w_up, rwkv_a0, rwkv_a_up, rwkv_g_up, rwkv_k_k, rwkv_k_a, rwkv_r_k,
                  rwkv_ln_w, rwkv_ln_b, mlstm_conv_w, mlstm_conv_b, mlstm_wq, mlstm_wk, mlstm_wv,
                  mlstm_w_gate, mlstm_b_gate, mlstm_gn_w, mlstm_skip, norm_final):
    depth = w_in.shape[0]
    pc, pr = 3 * D_CONV, P_RWKV
    row = lambda x: x[:, None, :]
    W = {}
    W["norm_ff"] = norm_ff[:, :, None, :]
    W["ffn_gate"] = ffn_gate.astype(BF16)
    W["ffn_up"] = ffn_up.astype(BF16)
    W["ffn_down"] = ffn_down.astype(BF16)
    W["norm_mix"] = row(norm_mix)
    W["w_in_c"] = w_in[:, :, :pc].astype(BF16)
    W["w_in_r"] = w_in[:, :, pc:pc + pr].astype(BF16)
    w_u = _pad_heads(w_in[:, :, pc + pr:pc + pr + D_MLSTM], 2)
    w_o = _pad_heads(w_in[:, :, pc + pr + D_MLSTM:], 2)
    W["w_in_m"] = jnp.concatenate([w_u, w_o], axis=2).astype(BF16)
    W["w_out_c"] = w_out[:, :D_CONV].astype(BF16)
    W["w_out_r"] = w_out[:, D_CONV:D_CONV + D_RWKV].astype(BF16)
    W["w_out_m"] = _pad_heads(w_out[:, D_CONV + D_RWKV:], 1).astype(BF16)
    W["conv_w"] = conv_w
    W["rwkv_mu"] = row(rwkv_mu)
    W["rwkv_w0"] = row(rwkv_w0)
    W["rwkv_a0"] = row(rwkv_a0)
    zl = lambda n: jnp.zeros((depth, n, D_RWKV), F32)
    W["rwkv_w_up"] = jnp.concatenate([rwkv_w_up, zl(LANES - LORA_W)], axis=1)
    W["rwkv_a_up"] = jnp.concatenate([zl(LORA_W), rwkv_a_up, zl(LORA_G)], axis=1)
    W["rwkv_g_up"] = jnp.concatenate([zl(LORA_W + LORA_A), rwkv_g_up], axis=1)
    W["rwkv_k_k"] = row(rwkv_k_k)
    W["rwkv_k_a"] = row(rwkv_k_a)
    W["rwkv_r_k"] = row(rwkv_r_k.reshape(depth, D_RWKV))
    W["rwkv_ln_w"] = row(rwkv_ln_w)
    W["rwkv_ln_b"] = row(rwkv_ln_b)
    W["mlstm_conv_w"] = _pad_heads(mlstm_conv_w, 2)
    W["mlstm_conv_b"] = row(_pad_heads(mlstm_conv_b, 1))
    W["mlstm_wq"] = _head_block_diag(mlstm_wq)
    W["mlstm_wk"] = _head_block_diag(mlstm_wk)
    W["mlstm_wv"] = _head_block_diag(mlstm_wv)
    wg = mlstm_w_gate.reshape(depth, H_M, 3, HD_M, 2 * H_M)
    for i, name in enumerate(("mlstm_wgq", "mlstm_wgk", "mlstm_wgv")):
        part = jnp.pad(wg[:, :, i], ((0, 0), (0, 0), (0, LANES - HD_M), (0, LANES - 2 * H_M)))
        W[name] = part.reshape(depth, D_MPAD, LANES)
    W["mlstm_b_gate"] = row(jnp.pad(mlstm_b_gate, ((0, 0), (0, LANES - 2 * H_M))))
    W["mlstm_gn_w"] = row(_pad_heads(mlstm_gn_w, 1))
    W["mlstm_skip"] = row(_pad_heads(mlstm_skip, 1))
    W["norm_final"] = norm_final[None, :]
    return W


_STATE_NAMES = ("conv", "shift", "wkv", "mconv", "mC", "mn", "mm")


def _run_trunk(x, states, W, t_valid):
    B, T, D = x.shape
    depth = W["w_in_c"].shape[0]
    st = dict(zip(_STATE_NAMES, states))
    st["mm"] = st["mm"][:, :, None, :]
    out = {n: jnp.zeros_like(s) for n, s in st.items()}
    x2 = x.reshape(B * T, D)
    seq = lambda a: a.reshape(B, T, a.shape[-1])
    flat = lambda a: a.reshape(B * T, a.shape[-1])
    for l in range(depth):
        x2 = _ffn(x2, W, l, 0)
        pc, pr, pm = _inproj(x2, W, l)
        yc, out["conv"] = _conv_mixer(seq(pc), st, out, W, l, t_valid)
        yr, out["shift"], out["wkv"] = _rwkv_mixer(seq(pr), st, out, W, l, t_valid)
        ym, out["mconv"], out["mC"], out["mn"], out["mm"] = _mlstm_mixer(seq(pm), st, out, W, l, t_valid)
        x2 = _mix_ffn(x2, flat(yc), flat(yr), flat(ym), W, l, final=(l == depth - 1))
    out["mm"] = out["mm"][:, :, 0, :]
    return x2.reshape(B, T, D), tuple(out[n] for n in _STATE_NAMES)


def kernel(x_prompt, x_sample, cache_conv, cache_shift, state_wkv, cache_mconv, state_mC, state_mn, state_mm, norm_ff, ffn_gate, ffn_up, ffn_down, norm_mix, w_in, w_out, conv_w, rwkv_mu, rwkv_w0, rwkv_w_up, rwkv_a0, rwkv_a_up, rwkv_g_up, rwkv_k_k, rwkv_k_a, rwkv_r_k, rwkv_ln_w, rwkv_ln_b, mlstm_conv_w, mlstm_conv_b, mlstm_wq, mlstm_wk, mlstm_wv, mlstm_w_gate, mlstm_b_gate, mlstm_gn_w, mlstm_skip, norm_final):
    W = _prep_weights(norm_ff, ffn_gate, ffn_up, ffn_down, norm_mix, w_in, w_out, conv_w,
                      rwkv_mu, rwkv_w0, rwkv_w_up, rwkv_a0, rwkv_a_up, rwkv_g_up, rwkv_k_k, rwkv_k_a, rwkv_r_k,
                      rwkv_ln_w, rwkv_ln_b, mlstm_conv_w, mlstm_conv_b, mlstm_wq, mlstm_wk, mlstm_wv,
                      mlstm_w_gate, mlstm_b_gate, mlstm_gn_w, mlstm_skip, norm_final)
    depth = w_in.shape[0]
    Bp, Tp, _ = x_prompt.shape
    Bs, Ts, _ = x_sample.shape
    zeros = lambda *shape: jnp.zeros((depth, Bp) + shape, F32)
    st_p = (zeros(CONV_W - 1, D_CONV), zeros(1, P_RWKV), zeros(H_R, HD_R, HD_R), zeros(MCONV_W - 1, D_MLSTM),
            zeros(H_M, HD_M, HD_M), zeros(H_M, HD_M), zeros(H_M))
    y_prompt, p_states = _run_trunk(x_prompt, st_p, W, Tp)

    st_s = (cache_conv, cache_shift, state_wkv, cache_mconv, state_mC, state_mn, state_mm)
    xs = jnp.pad(x_sample, ((0, 0), (0, SAMPLE_PAD_T - Ts), (0, 0)))
    y_sample, s_states = _run_trunk(xs, st_s, W, Ts)
    y_sample = y_sample[:, :Ts]
    return (y_prompt, y_sample) + p_states + s_states
```

```python
import functools
import math

import jax
import jax.numpy as jnp
from jax import lax
from jax.experimental import pallas as pl
from jax.experimental.pallas import tpu as pltpu

F32 = jnp.float32
BF16 = jnp.bfloat16

LANES = 128
H_R, HD_R = 6, 64
D_RWKV = H_R * HD_R
N_PAIR = H_R // 2
LORA_W, LORA_A, LORA_G = 32, 32, 64
P_RWKV = 3 * D_RWKV + LORA_W + LORA_A + LORA_G
H_M, HD_M = 4, 96
D_MLSTM = H_M * HD_M
D_MPAD = H_M * LANES
D_CONV = 256
CONV_W, MCONV_W = 3, 4
RMS_EPS = 1e-6
GN_EPS_R = 64e-5
GN_EPS_M = 1e-6
SHORT_SEQ_ROWS = 8
RWKV_CHUNK = 64
RWKV_STEP_ROWS = 256
MLSTM_CHUNK = 256
MLSTM_STEP_ROWS = 256
MLSTM_ROW_BLOCK = 128
CONV_CHUNK = 512
MAX_SEQS_PER_STEP = 8
TOKEN_TILE = 512
FFN_COLS = 256
VMEM_LIMIT = 48 * 1024 * 1024

NN = ((1,), (0,))
NT = ((1,), (1,))
TN = ((0,), (0,))

MODE_PROJ = "b1"
MODE_SCAN = "b1"


def _mm(a, b, dims=NN, mode="b1"):
    dn = (dims, ((), ()))
    dot = lambda x, y: lax.dot_general(x, y, dn, preferred_element_type=F32)
    ah = a.astype(BF16)
    bh = b.astype(BF16)
    if mode == "b1":
        return dot(ah, bh)
    al = (a - ah.astype(F32)).astype(BF16)
    bl = (b - bh.astype(F32)).astype(BF16)
    return dot(ah, bh) + (dot(ah, bl) + dot(al, bh))


def _rms(x, g):
    return x * lax.rsqrt(jnp.mean(x * x, axis=-1, keepdims=True) + RMS_EPS) * g


def _sigmoid(x):
    return 1.0 / (1.0 + jnp.exp(-x))


def _softplus(x):
    return jnp.maximum(x, 0.0) + jnp.log1p(jnp.exp(-jnp.abs(x)))


def _masked_sum(mask, x):
    m = mask.astype(BF16)
    x1 = x.astype(BF16)
    r1 = x - x1.astype(F32)
    x2 = r1.astype(BF16)
    x3 = (r1 - x2.astype(F32)).astype(BF16)
    dot = lambda y: jnp.dot(m, y, preferred_element_type=F32)
    return dot(x1) + (dot(x2) + dot(x3))


def _seq_rows(ref, g, L):
    x = ref[g]
    if x.shape[0] < L:
        x = jnp.concatenate([x, jnp.zeros((L - x.shape[0], x.shape[1]), x.dtype)], axis=0)
    return x


def _cat_rows(xs):
    return xs[0] if len(xs) == 1 else jnp.concatenate(xs, axis=0)


def _half_ffn(x, g_ref, wg_ref, wu_ref, wd_ref):
    n = _rms(x, g_ref[...]).astype(BF16)
    acc = jnp.zeros(x.shape, F32)
    for c in range(wg_ref.shape[1] // FFN_COLS):
        sl = slice(c * FFN_COLS, (c + 1) * FFN_COLS)
        hg = jnp.dot(n, wg_ref[:, sl], preferred_element_type=F32)
        hu = jnp.dot(n, wu_ref[:, sl], preferred_element_type=F32)
        h = (hg * _sigmoid(hg) * hu).astype(BF16)
        acc = acc + jnp.dot(h, wd_ref[sl, :], preferred_element_type=F32)
    return x + 0.5 * acc


def _ffn_body(x_ref, g_ref, wg_ref, wu_ref, wd_ref, o_ref):
    o_ref[...] = _half_ffn(x_ref[...], g_ref, wg_ref, wu_ref, wd_ref)


def _mix_ffn_body(x_ref, yc_ref, yr_ref, ym_ref, wc_ref, wr_ref, wm_ref, g_ref, wg_ref, wu_ref, wd_ref,
                  gf_ref, o_ref, *, final):
    acc = jnp.dot(yc_ref[...].astype(BF16), wc_ref[...], preferred_element_type=F32)
    acc = acc + jnp.dot(yr_ref[...].astype(BF16), wr_ref[...], preferred_element_type=F32)
    acc = acc + jnp.dot(ym_ref[...].astype(BF16), wm_ref[...], preferred_element_type=F32)
    x = _half_ffn(x_ref[...] + acc, g_ref, wg_ref, wu_ref, wd_ref)
    o_ref[...] = _rms(x, gf_ref[...]) if final else x


def _inproj_body(x_ref, g_ref, wc_ref, wr_ref, wm_ref, pc_ref, pr_ref, pm_ref):
    n = _rms(x_ref[...], g_ref[...]).astype(BF16)
    pc_ref[...] = jnp.dot(n, wc_ref[...], preferred_element_type=F32)
    pr_ref[...] = jnp.dot(n, wr_ref[...], preferred_element_type=F32)
    pm_ref[...] = jnp.dot(n, wm_ref[...], preferred_element_type=F32)


def _rows(tm, width):
    return pl.BlockSpec((tm, width), lambda i: (i, 0))


def _resident(shape, *lead):
    nlead = len(lead)
    zeros = (0,) * len(shape)
    return pl.BlockSpec((None,) * nlead + tuple(shape), lambda i: tuple(lead) + zeros,
                        pipeline_mode=pl.Buffered(1))


def _token_params():
    return pltpu.CompilerParams(dimension_semantics=("parallel",), vmem_limit_bytes=VMEM_LIMIT)


def _ffn(x2, W, l, which):
    n, d = x2.shape
    f = W["ffn_gate"].shape[-1]
    tm = min(TOKEN_TILE, n)
    return pl.pallas_call(
        _ffn_body,
        grid=(n // tm,),
        in_specs=[_rows(tm, d), _resident((1, d), l, which), _resident((d, f), l, which),
                  _resident((d, f), l, which), _resident((f, d), l, which)],
        out_specs=_rows(tm, d),
        out_shape=jax.ShapeDtypeStruct((n, d), F32),
        compiler_params=_token_params(),
        name="ffn",
    )(x2, W["norm_ff"], W["ffn_gate"], W["ffn_up"], W["ffn_down"])


def _inproj(x2, W, l):
    n, d = x2.shape
    tm = min(TOKEN_TILE, n)
    wc, wr, wm = W["w_in_c"], W["w_in_r"], W["w_in_m"]
    widths = (wc.shape[-1], wr.shape[-1], wm.shape[-1])
    return pl.pallas_call(
        _inproj_body,
        grid=(n // tm,),
        in_specs=[_rows(tm, d), _resident((1, d), l)] + [_resident((d, w), l) for w in widths],
        out_specs=[_rows(tm, w) for w in widths],
        out_shape=[jax.ShapeDtypeStruct((n, w), F32) for w in widths],
        compiler_params=_token_params(),
        name="inproj",
    )(x2, W["norm_mix"], wc, wr, wm)


def _mix_ffn(x2, yc, yr, ym, W, l, final):
    n, d = x2.shape
    f = W["ffn_gate"].shape[-1]
    tm = min(TOKEN_TILE, n)
    ws = (W["w_out_c"], W["w_out_r"], W["w_out_m"])
    return pl.pallas_call(
        functools.partial(_mix_ffn_body, final=final),
        grid=(n // tm,),
        in_specs=[_rows(tm, d)] + [_rows(tm, w.shape[1]) for w in ws] + [_resident(w.shape[1:], l) for w in ws]
                 + [_resident((1, d), l, 1), _resident((d, f), l, 1), _resident((d, f), l, 1),
                    _resident((f, d), l, 1), _resident((1, d))],
        out_specs=_rows(tm, d),
        out_shape=jax.ShapeDtypeStruct((n, d), F32),
        compiler_params=_token_params(),
        name="mix_ffn",
    )(x2, yc, yr, ym, *ws, W["norm_ff"], W["ffn_gate"], W["ffn_up"], W["ffn_down"], W["norm_final"])


def _conv_body(pc_ref, buf_ref, w_ref, y_ref, nb_ref, xp_scr, *, G, L, lv):
    c = pl.program_id(1)
    hist = CONV_W - 1
    w = w_ref[...]
    for g in range(G):
        @pl.when(c == 0)
        def _():
            xp_scr[g, 8 - hist:8, :] = buf_ref[g]

        p = _seq_rows(pc_ref, g, L)
        b_gate = p[:, :D_CONV]
        xp_scr[g, 8:8 + L, :] = p[:, D_CONV:2 * D_CONV] * p[:, 2 * D_CONV:]
        z = w[0:1] * xp_scr[g, 6:6 + L, :]
        z = z + w[1:2] * xp_scr[g, 7:7 + L, :]
        z = z + w[2:3] * xp_scr[g, 8:8 + L, :]
        y_ref[g] = (b_gate * z)[:y_ref.shape[1]]

        @pl.when(c == pl.num_programs(1) - 1)
        def _():
            nb_ref[g] = xp_scr[g, 8 + lv - hist:8 + lv, :]

        tail = xp_scr[g, 8 + L - hist:8 + L, :]
        xp_scr[g, 8 - hist:8, :] = tail


def _rwkv_body(pr_ref, sh_ref, s0_ref, mu_ref, w0_ref, wup_ref, a0_ref, aup_ref, gup_ref, kk_ref, ka_ref,
               rk_ref, lnw_ref, lnb_ref, y_ref, sho_ref, so_ref, xp_scr, s_scr, *, G, L, lv):
    c = pl.program_id(1)
    GL = G * L
    shift = L.bit_length() - 1
    seqs = range(G)

    @pl.when(c == 0)
    def _():
        for g in seqs:
            xp_scr[g, 7:8, :] = sh_ref[g]
            for j in range(N_PAIR):
                s_scr[g, j] = jnp.zeros((LANES, LANES), F32)
                s_scr[g, j, :HD_R, :HD_R] = s0_ref[g, 2 * j]
                s_scr[g, j, HD_R:, HD_R:] = s0_ref[g, 2 * j + 1]

    p_seq = [_seq_rows(pr_ref, g, L) for g in seqs]
    for g in seqs:
        xp_scr[g, 8:8 + L, :] = p_seq[g]
    p = _cat_rows(p_seq)
    prev = _cat_rows([xp_scr[g, 7:7 + L, :] for g in seqs])
    ps = p + (prev - p) * mu_ref[...]
    r = ps[:, 0:D_RWKV]
    k = ps[:, D_RWKV:2 * D_RWKV]
    v = ps[:, 2 * D_RWKV:3 * D_RWKV]
    x4 = ps[:, 3 * D_RWKV:]
    lw = -math.exp(-0.5) * _sigmoid(w0_ref[...] + _mm(jnp.tanh(x4), wup_ref[...], NN, MODE_PROJ))
    a = _sigmoid(a0_ref[...] + _mm(x4, aup_ref[...], NN, MODE_PROJ))
    gate = _mm(_sigmoid(x4), gup_ref[...], NN, MODE_PROJ)

    lane = lax.broadcasted_iota(jnp.int32, (1, LANES), 1)
    lo = lane < HD_R

    def pair_sum(x):
        s0 = jnp.sum(jnp.where(lo, x, 0.0), axis=-1, keepdims=True)
        s1 = jnp.sum(jnp.where(lo, 0.0, x), axis=-1, keepdims=True)
        return jnp.where(lo, s0, s1)

    def head_sum(x):
        return jnp.concatenate([pair_sum(x[:, j * LANES:(j + 1) * LANES]) for j in range(N_PAIR)], axis=1)

    kk = k * kk_ref[...]
    kk = kk * lax.rsqrt(jnp.maximum(head_sum(kk * kk), 1e-24))
    k2 = k * (1.0 + (a - 1.0) * ka_ref[...])
    bonus = head_sum(r * k2 * rk_ref[...]) * v

    if lv < L:
        valid = (lax.broadcasted_iota(jnp.int32, (GL, 1), 0) & (L - 1)) < lv
        lw = jnp.where(valid, lw, 0.0)
        kk = jnp.where(valid, kk, 0.0)
        k2 = jnp.where(valid, k2, 0.0)

    ri = lax.broadcasted_iota(jnp.int32, (GL, GL), 0)
    ci = lax.broadcasted_iota(jnp.int32, (GL, GL), 1)
    tri = jnp.logical_and(ci <= ri, (ci >> shift) == (ri >> shift))
    cl = _masked_sum(tri, lw)
    e_pos = jnp.exp(cl)
    e_neg = jnp.exp(-cl)
    a_t = -kk * jnp.exp(cl - lw)
    b_t = kk * a * e_neg
    k_t = k2 * e_neg
    r_t = r * e_pos

    L2 = 2 * L
    ri2 = lax.broadcasted_iota(jnp.int32, (L2, L2), 0)
    ci2 = lax.broadcasted_iota(jnp.int32, (L2, L2), 1)
    tri2 = ri2 & (L - 1)
    tci2 = ci2 & (L - 1)
    strict = tci2 < tri2
    incl = tci2 <= tri2
    eye = (ri2 == ci2).astype(F32)
    row_lo = lax.broadcasted_iota(jnp.int32, (L2, 1), 0) < L
    own = jnp.logical_and(row_lo, lo) | jnp.logical_and(jnp.logical_not(row_lo), jnp.logical_not(lo))

    def stack(x):
        return jnp.concatenate([jnp.where(lo, x, 0.0), jnp.where(lo, 0.0, x)], axis=0)

    units = [(g, j) for g in seqs for j in range(N_PAIR)]
    U = range(len(units))
    rsl = [slice(g * L, (g + 1) * L) for g, _ in units]
    lsl = [slice(j * LANES, (j + 1) * LANES) for _, j in units]
    mm = functools.partial(_mm, mode=MODE_SCAN)
    xa, xb, xk, xr, xv = ([stack(t[rsl[i], lsl[i]]) for i in U] for t in (a_t, b_t, k_t, r_t, v))
    s_bd = [s_scr[g, j] for g, j in units]
    xar = [jnp.concatenate([xa[i], xr[i]], axis=0) for i in U]
    gram = [mm(xar[i], jnp.concatenate([xb[i], xk[i]], axis=0), NT) for i in U]
    a_ab = [jnp.where(strict, gm[:L2, :L2], 0.0) for gm in gram]
    a_ak = [jnp.where(strict, gm[:L2, L2:], 0.0) for gm in gram]
    m_rb = [jnp.where(incl, gm[L2:, :L2], 0.0) for gm in gram]
    m_rk = [jnp.where(incl, gm[L2:, L2:], 0.0) for gm in gram]
    sh = [mm(xar[i], s_bd[i], NT) for i in U]
    akv = [sh[i] + mm(jnp.concatenate([a_ak[i], m_rk[i]], axis=0), xv[i], NN) for i in U]
    tinv = [eye + m for m in a_ab]
    steps = L.bit_length() - 2
    if steps > 0:
        pw = [mm(m, m, NN) for m in a_ab]
    for step in range(steps):
        if step < steps - 1:
            both = [mm(pw[i], jnp.concatenate([tinv[i], pw[i]], axis=1), NN) for i in U]
            tinv = [tinv[i] + both[i][:, :L2] for i in U]
            pw = [both[i][:, L2:] for i in U]
        else:
            tinv = [tinv[i] + mm(pw[i], tinv[i], NN) for i in U]
    u = [mm(tinv[i], akv[i][:L2], NN) for i in U]
    o = [akv[i][L2:] + mm(m_rb[i], u[i], NN) for i in U]
    for i, (g, j) in enumerate(units):
        g_last = e_pos[g * L + L - 1:g * L + L, lsl[i]]
        upd = mm(jnp.concatenate([u[i], xv[i]], axis=0), jnp.concatenate([xb[i], xk[i]], axis=0), TN)
        s_scr[g, j] = (s_bd[i] + upd) * g_last
    outs = []
    for i in U:
        mean = jnp.sum(o[i], axis=-1, keepdims=True) * (1.0 / HD_R)
        oc = jnp.where(own, o[i] - mean, 0.0)
        var = jnp.sum(oc * oc, axis=-1, keepdims=True) * (1.0 / HD_R)
        on = oc * lax.rsqrt(var + GN_EPS_R)
        outs.append(on[:L] + on[L:])

    yn = _cat_rows([jnp.concatenate(outs[g * N_PAIR:(g + 1) * N_PAIR], axis=1) for g in seqs])
    y = (yn * lnw_ref[...] + lnb_ref[...] + bonus) * gate
    for g in seqs:
        y_ref[g] = y[g * L:g * L + y_ref.shape[1]]

    @pl.when(c == pl.num_programs(1) - 1)
    def _():
        for g in seqs:
            sho_ref[g] = xp_scr[g, 8 + lv - 1:8 + lv, :]
            for j in range(N_PAIR):
                so_ref[g, 2 * j] = s_scr[g, j, :HD_R, :HD_R]
                so_ref[g, 2 * j + 1] = s_scr[g, j, HD_R:, HD_R:]

    for g in seqs:
        xp_scr[g, 7:8, :] = xp_scr[g, 7 + L:8 + L, :]


def _mlstm_body(pm_ref, cb_ref, c0_ref, n0_ref, m0_ref, cw_ref, cbias_ref, wq_ref, wk_ref, wv_ref,
                wgq_ref, wgk_ref, wgv_ref, bg_ref, gnw_ref, skip_ref,
                y_ref, cbo_ref, co_ref, no_ref, mo_ref, xp_scr, c_scr, m_scr, *, G, L, lv):
    c = pl.program_id(1)
    hist = MCONV_W - 1
    GL = G * L
    shift = L.bit_length() - 1
    seqs = range(G)

    @pl.when(c == 0)
    def _():
        for g in seqs:
            xp_scr[g, 0:8, :] = jnp.zeros((8, D_MPAD), F32)
            cb = cb_ref[g]
            m0 = m0_ref[g]
            n0 = n0_ref[g]
            for h in range(H_M):
                xp_scr[g, 8 - hist:8, h * LANES:h * LANES + HD_M] = cb[:, h * HD_M:(h + 1) * HD_M]
                c_scr[g, h] = jnp.zeros((LANES, LANES), F32)
                c_scr[g, h, :HD_M, :HD_M] = c0_ref[g, h]
                c_scr[g, h, HD_M:HD_M + 1, :HD_M] = n0[h:h + 1, :]
                m_scr[g, h:h + 1, :] = jnp.broadcast_to(m0[:, h:h + 1], (1, LANES))

    cw = cw_ref[...]
    pm_seq = [_seq_rows(pm_ref, g, L) for g in seqs]
    ucs = []
    for g in seqs:
        u_g = pm_seq[g][:, :D_MPAD]
        xp_scr[g, 8:8 + L, :] = u_g
        acc = cw[0:1] * xp_scr[g, 5:5 + L, :]
        acc = acc + cw[1:2] * xp_scr[g, 6:6 + L, :]
        acc = acc + cw[2:3] * xp_scr[g, 7:7 + L, :]
        ucs.append(acc + cw[3:4] * u_g)
    pm = _cat_rows(pm_seq)
    u = pm[:, :D_MPAD]
    og = pm[:, D_MPAD:]
    uc = _cat_rows(ucs) + cbias_ref[...]
    uc = uc * _sigmoid(uc)
    q = _mm(uc, wq_ref[...], NN, MODE_PROJ)
    k = _mm(uc, wk_ref[...], NN, MODE_PROJ)
    v = _mm(u, wv_ref[...], NN, MODE_PROJ)
    gp = (_mm(q, wgq_ref[...], NN, MODE_PROJ) + _mm(k, wgk_ref[...], NN, MODE_PROJ)
          + _mm(v, wgv_ref[...], NN, MODE_PROJ)) + bg_ref[...]
    k = k * (HD_M ** -0.5)

    ig = gp
    lf = -_softplus(-gp)
    if lv < L:
        valid = (lax.broadcasted_iota(jnp.int32, (GL, 1), 0) & (L - 1)) < lv
        ig = jnp.where(valid, ig, -jnp.inf)
        lf = jnp.where(valid, lf, 0.0)
    rg = lax.broadcasted_iota(jnp.int32, (GL, GL), 0)
    cg = lax.broadcasted_iota(jnp.int32, (GL, GL), 1)
    tri = jnp.logical_and(cg <= rg, (cg >> shift) == (rg >> shift))
    bcum = _masked_sum(tri, lf)
    lane = lax.broadcasted_iota(jnp.int32, (1, LANES), 1)
    gates = jnp.where(lane < H_M, ig, bcum)
    gates_t = gates.T
    head_lane = lane < HD_M
    one_lane = lane == HD_M
    ri = lax.broadcasted_iota(jnp.int32, (L, L), 0)
    ci = lax.broadcasted_iota(jnp.int32, (L, L), 1)
    causal = ci <= ri

    units = [(g, h) for g in seqs for h in range(H_M)]
    U = range(len(units))
    mm = functools.partial(_mm, mode=MODE_SCAN)
    rsl = [slice(g * L, (g + 1) * L) for g, _ in units]
    lsl = [slice(h * LANES, (h + 1) * LANES) for _, h in units]
    qh = [q[rsl[i], lsl[i]] for i in U]
    kh = [k[rsl[i], lsl[i]] for i in U]
    vh = [jnp.where(one_lane, 1.0, v[rsl[i], lsl[i]]) for i in U]
    i_c = [gates[rsl[i], h:h + 1] for i, (g, h) in enumerate(units)]
    b_c = [gates[rsl[i], H_M + h:H_M + h + 1] for i, (g, h) in enumerate(units)]
    i_r = [gates_t[h:h + 1, rsl[i]] for i, (g, h) in enumerate(units)]
    b_r = [gates_t[H_M + h:H_M + h + 1, rsl[i]] for i, (g, h) in enumerate(units)]
    m_prev = [m_scr[g, h:h + 1, 0:1] for g, h in units]
    c_h = [c_scr[g, h] for g, h in units]
    qc = [mm(qh[i], c_h[i], NT) for i in U]
    rowv = [i_r[i] - b_r[i] for i in U]
    inter = [b_c[i] + m_prev[i] for i in U]
    RB = min(L, MLSTM_ROW_BLOCK)
    hh = [[] for _ in U]
    m_last = [None for _ in U]
    for rb in range(L // RB):
        r0, r1 = rb * RB, (rb + 1) * RB
        cmask = causal[r0:r1, :r1]
        dm = [jnp.where(cmask, b_c[i][r0:r1] + rowv[i][:, :r1], -jnp.inf) for i in U]
        m_t = [jnp.maximum(inter[i][r0:r1], jnp.max(dm[i], axis=-1, keepdims=True)) for i in U]
        qk = [mm(qh[i][r0:r1], kh[i][:r1], NT) for i in U]
        s = [qk[i] * jnp.exp(dm[i] - m_t[i]) for i in U]
        w_i = [jnp.exp(inter[i][r0:r1] - m_t[i]) for i in U]
        num = [mm(s[i], vh[i][:r1], NN) + w_i[i] * qc[i][r0:r1] for i in U]
        for i in U:
            den = num[i][:, HD_M:HD_M + 1]
            hh[i].append(jnp.where(head_lane, num[i], 0.0) / jnp.maximum(jnp.abs(den), jnp.exp(-m_t[i])))
            m_last[i] = m_t[i][RB - 1:RB]
    hh = [_cat_rows(h) for h in hh]
    outs = []
    for i, (g, h) in enumerate(units):
        b_l = b_c[i][L - 1:L]
        m_new = m_last[i]
        w_s = jnp.exp(b_l - b_c[i] + i_c[i] - m_new)
        w_p = jnp.exp(b_l + m_prev[i] - m_new)
        ks = w_s * kh[i]
        c_scr[g, h] = w_p * c_h[i] + mm(vh[i], ks, TN)
        m_scr[g, h:h + 1, :] = jnp.broadcast_to(m_new, (1, LANES))
        mean = jnp.sum(hh[i], axis=-1, keepdims=True) * (1.0 / HD_M)
        hc = jnp.where(head_lane, hh[i] - mean, 0.0)
        var = jnp.sum(hc * hc, axis=-1, keepdims=True) * (1.0 / HD_M)
        outs.append(hc * lax.rsqrt(var + GN_EPS_M))

    hn = _cat_rows([jnp.concatenate(outs[g * H_M:(g + 1) * H_M], axis=1) for g in seqs])
    y = (hn * gnw_ref[...] + skip_ref[...] * uc) * _sigmoid(og)
    for g in seqs:
        y_ref[g] = y[g * L:g * L + y_ref.shape[1]]

    @pl.when(c == pl.num_programs(1) - 1)
    def _():
        lane_h = lax.broadcasted_iota(jnp.int32, (1, H_M), 1)
        for g in seqs:
            rows = xp_scr[g, 8 + lv - hist:8 + lv, :]
            cbo_ref[g] = jnp.concatenate([rows[:, h * LANES:h * LANES + HD_M] for h in range(H_M)], axis=1)
            no_ref[g] = jnp.concatenate([c_scr[g, h, HD_M:HD_M + 1, :HD_M] for h in range(H_M)], axis=0)
            m_row = jnp.zeros((1, H_M), F32)
            for h in range(H_M):
                co_ref[g, h] = c_scr[g, h, :HD_M, :HD_M]
                m_row = jnp.where(lane_h == h, m_scr[g, h:h + 1, 0:1], m_row)
            mo_ref[g] = m_row

    for g in seqs:
        tail = xp_scr[g, 8 + L - hist:8 + L, :]
        xp_scr[g, 8 - hist:8, :] = tail


def _seq_spec(G, L, width):
    return pl.BlockSpec((G, L, width), lambda b, c: (b, c, 0))


def _state_spec(G, shape, l):
    zeros = (0,) * len(shape)
    return pl.BlockSpec((None, G) + tuple(shape), lambda b, c: (l, b) + zeros)


def _param_spec(shape, l):
    zeros = (0,) * len(shape)
    return pl.BlockSpec((None,) + tuple(shape), lambda b, c: (l,) + zeros)


def _mixer_params():
    return pltpu.CompilerParams(dimension_semantics=("parallel", "arbitrary"), vmem_limit_bytes=VMEM_LIMIT)


def _chunking(B, T, chunk, step_rows):
    if T < SHORT_SEQ_ROWS:
        tb, L, nc = T, SHORT_SEQ_ROWS, 1
    else:
        tb = L = min(T, chunk)
        nc = T // L
        assert T % L == 0
    assert L & (L - 1) == 0
    G = max(1, min(B, step_rows // L, MAX_SEQS_PER_STEP))
    assert B % G == 0
    return tb, L, nc, G


def _conv_mixer(pc, st, W, l):
    B, T, _ = pc.shape
    tb, L, nc, G = _chunking(B, T, CONV_CHUNK, CONV_CHUNK)
    hist = CONV_W - 1
    state = _state_spec(G, (hist, D_CONV), l)
    return pl.pallas_call(
        functools.partial(_conv_body, G=G, L=L, lv=tb),
        grid=(B // G, nc),
        in_specs=[_seq_spec(G, tb, 3 * D_CONV), state, _param_spec((CONV_W, D_CONV), l)],
        out_specs=[_seq_spec(G, tb, D_CONV), state],
        out_shape=[jax.ShapeDtypeStruct((B, T, D_CONV), F32), jax.ShapeDtypeStruct(st["conv"].shape, F32)],
        input_output_aliases={1: 1},
        scratch_shapes=[pltpu.VMEM((G, 8 + L, D_CONV), F32)],
        compiler_params=_mixer_params(),
        name="short_conv",
    )(pc, st["conv"], W["conv_w"])


def _rwkv_mixer(pr, st, W, l):
    B, T, _ = pr.shape
    tb, L, nc, G = _chunking(B, T, RWKV_CHUNK, RWKV_STEP_ROWS)
    vec = lambda n: _param_spec((1, n), l)
    shift = _state_spec(G, (1, P_RWKV), l)
    wkv = _state_spec(G, (H_R, HD_R, HD_R), l)
    return pl.pallas_call(
        functools.partial(_rwkv_body, G=G, L=L, lv=tb),
        grid=(B // G, nc),
        in_specs=[_seq_spec(G, tb, P_RWKV), shift, wkv,
                  vec(P_RWKV), vec(D_RWKV), _param_spec((LANES, D_RWKV), l), vec(D_RWKV),
                  _param_spec((LANES, D_RWKV), l), _param_spec((LANES, D_RWKV), l),
                  vec(D_RWKV), vec(D_RWKV), vec(D_RWKV), vec(D_RWKV), vec(D_RWKV)],
        out_specs=[_seq_spec(G, tb, D_RWKV), shift, wkv],
        out_shape=[jax.ShapeDtypeStruct((B, T, D_RWKV), F32), jax.ShapeDtypeStruct(st["shift"].shape, F32),
                   jax.ShapeDtypeStruct(st["wkv"].shape, F32)],
        input_output_aliases={1: 1, 2: 2},
        scratch_shapes=[pltpu.VMEM((G, 8 + L, P_RWKV), F32), pltpu.VMEM((G, N_PAIR, LANES, LANES), F32)],
        compiler_params=_mixer_params(),
        name="rwkv7",
    )(pr, st["shift"], st["wkv"], W["rwkv_mu"], W["rwkv_w0"], W["rwkv_w_up"], W["rwkv_a0"], W["rwkv_a_up"],
      W["rwkv_g_up"], W["rwkv_k_k"], W["rwkv_k_a"], W["rwkv_r_k"], W["rwkv_ln_w"], W["rwkv_ln_b"])


def _mlstm_mixer(pm, st, W, l):
    B, T, _ = pm.shape
    tb, L, nc, G = _chunking(B, T, MLSTM_CHUNK, MLSTM_STEP_ROWS)
    hist = MCONV_W - 1
    vec = lambda n: _param_spec((1, n), l)
    sq = _param_spec((D_MPAD, D_MPAD), l)
    gate = _param_spec((D_MPAD, LANES), l)
    states = [_state_spec(G, (hist, D_MLSTM), l), _state_spec(G, (H_M, HD_M, HD_M), l),
              _state_spec(G, (H_M, HD_M), l), _state_spec(G, (1, H_M), l)]
    names = ("mconv", "mC", "mn", "mm")
    return pl.pallas_call(
        functools.partial(_mlstm_body, G=G, L=L, lv=tb),
        grid=(B // G, nc),
        in_specs=[_seq_spec(G, tb, 2 * D_MPAD)] + states
                 + [_param_spec((MCONV_W, D_MPAD), l), vec(D_MPAD), sq, sq, sq, gate, gate, gate, vec(LANES),
                    vec(D_MPAD), vec(D_MPAD)],
        out_specs=[_seq_spec(G, tb, D_MPAD)] + states,
        out_shape=[jax.ShapeDtypeStruct((B, T, D_MPAD), F32)]
                  + [jax.ShapeDtypeStruct(st[n].shape, F32) for n in names],
        input_output_aliases={1: 1, 2: 2, 3: 3, 4: 4},
        scratch_shapes=[pltpu.VMEM((G, 8 + L, D_MPAD), F32), pltpu.VMEM((G, H_M, LANES, LANES), F32),
                        pltpu.VMEM((G, H_M, LANES), F32)],
        compiler_params=_mixer_params(),
        name="mlstm",
    )(pm, st["mconv"], st["mC"], st["mn"], st["mm"], W["mlstm_conv_w"], W["mlstm_conv_b"], W["mlstm_wq"],
      W["mlstm_wk"], W["mlstm_wv"], W["mlstm_wgq"], W["mlstm_wgk"], W["mlstm_wgv"], W["mlstm_b_gate"],
      W["mlstm_gn_w"], W["mlstm_skip"])


def _pad_heads(x, axis):
    axis = axis % x.ndim
    shp = x.shape
    x = x.reshape(shp[:axis] + (H_M, HD_M) + shp[axis + 1:])
    pad = [(0, 0)] * x.ndim
    pad[axis + 1] = (0, LANES - HD_M)
    x = jnp.pad(x, pad)
    return x.reshape(shp[:axis] + (D_MPAD,) + shp[axis + 1:])


def _head_block_diag(w):
    wp = jnp.pad(w, ((0, 0), (0, 0), (0, LANES - HD_M), (0, LANES - HD_M)))
    eye = jnp.eye(H_M, dtype=w.dtype)
    return jnp.einsum("lhde,hg->lhdge", wp, eye).reshape(w.shape[0], D_MPAD, D_MPAD)


def _prep_weights(norm_ff, ffn_gate, ffn_up, ffn_down, norm_mix, w_in, w_out, conv_w,
                  rwkv_mu, rwkv_w0, rwkv_w_up, rwkv_a0, rwkv_a_up, rwkv_g_up, rwkv_k_k, rwkv_k_a, rwkv_r_k,
                  rwkv_ln_w, rwkv_ln_b, mlstm_conv_w, mlstm_conv_b, mlstm_wq, mlstm_wk, mlstm_wv,
                  mlstm_w_gate, mlstm_b_gate, mlstm_gn_w, mlstm_skip, norm_final):
    depth = w_in.shape[0]
    pc, pr = 3 * D_CONV, P_RWKV
    row = lambda x: x[:, None, :]
    W = {}
    W["norm_ff"] = norm_ff[:, :, None, :]
    W["ffn_gate"] = ffn_gate.astype(BF16)
    W["ffn_up"] = ffn_up.astype(BF16)
    W["ffn_down"] = ffn_down.astype(BF16)
    W["norm_mix"] = row(norm_mix)
    W["w_in_c"] = w_in[:, :, :pc].astype(BF16)
    W["w_in_r"] = w_in[:, :, pc:pc + pr].astype(BF16)
    w_u = _pad_heads(w_in[:, :, pc + pr:pc + pr + D_MLSTM], 2)
    w_o = _pad_heads(w_in[:, :, pc + pr + D_MLSTM:], 2)
    W["w_in_m"] = jnp.concatenate([w_u, w_o], axis=2).astype(BF16)
    W["w_out_c"] = w_out[:, :D_CONV].astype(BF16)
    W["w_out_r"] = w_out[:, D_CONV:D_CONV + D_RWKV].astype(BF16)
    W["w_out_m"] = _pad_heads(w_out[:, D_CONV + D_RWKV:], 1).astype(BF16)
    W["conv_w"] = conv_w
    W["rwkv_mu"] = row(rwkv_mu)
    W["rwkv_w0"] = row(rwkv_w0)
    W["rwkv_a0"] = row(rwkv_a0)
    zl = lambda n: jnp.zeros((depth, n, D_RWKV), F32)
    W["rwkv_w_up"] = jnp.concatenate([rwkv_w_up, zl(LANES - LORA_W)], axis=1)
    W["rwkv_a_up"] = jnp.concatenate([zl(LORA_W), rwkv_a_up, zl(LORA_G)], axis=1)
    W["rwkv_g_up"] = jnp.concatenate([zl(LORA_W + LORA_A), rwkv_g_up], axis=1)
    W["rwkv_k_k"] = row(rwkv_k_k)
    W["rwkv_k_a"] = row(rwkv_k_a)
    W["rwkv_r_k"] = row(rwkv_r_k.reshape(depth, D_RWKV))
    W["rwkv_ln_w"] = row(rwkv_ln_w)
    W["rwkv_ln_b"] = row(rwkv_ln_b)
    W["mlstm_conv_w"] = _pad_heads(mlstm_conv_w, 2)
    W["mlstm_conv_b"] = row(_pad_heads(mlstm_conv_b, 1))
    W["mlstm_wq"] = _head_block_diag(mlstm_wq)
    W["mlstm_wk"] = _head_block_diag(mlstm_wk)
    W["mlstm_wv"] = _head_block_diag(mlstm_wv)
    wg = mlstm_w_gate.reshape(depth, H_M, 3, HD_M, 2 * H_M)
    for i, name in enumerate(("mlstm_wgq", "mlstm_wgk", "mlstm_wgv")):
        part = jnp.pad(wg[:, :, i], ((0, 0), (0, 0), (0, LANES - HD_M), (0, LANES - 2 * H_M)))
        W[name] = part.reshape(depth, D_MPAD, LANES)
    W["mlstm_b_gate"] = row(jnp.pad(mlstm_b_gate, ((0, 0), (0, LANES - 2 * H_M))))
    W["mlstm_gn_w"] = row(_pad_heads(mlstm_gn_w, 1))
    W["mlstm_skip"] = row(_pad_heads(mlstm_skip, 1))
    W["norm_final"] = norm_final[None, :]
    return W


_STATE_NAMES = ("conv", "shift", "wkv", "mconv", "mC", "mn", "mm")


def _run_trunk(x, states, W):
    B, T, D = x.shape
    depth = W["w_in_c"].shape[0]
    st = dict(zip(_STATE_NAMES, states))
    st["mm"] = st["mm"][:, :, None, :]
    x2 = x.reshape(B * T, D)
    seq = lambda a: a.reshape(B, T, a.shape[-1])
    flat = lambda a: a.reshape(B * T, a.shape[-1])
    for l in range(depth):
        x2 = _ffn(x2, W, l, 0)
        pc, pr, pm = _inproj(x2, W, l)
        yc, st["conv"] = _conv_mixer(seq(pc), st, W, l)
        yr, st["shift"], st["wkv"] = _rwkv_mixer(seq(pr), st, W, l)
        ym, st["mconv"], st["mC"], st["mn"], st["mm"] = _mlstm_mixer(seq(pm), st, W, l)
        x2 = _mix_ffn(x2, flat(yc), flat(yr), flat(ym), W, l, final=(l == depth - 1))
    st["mm"] = st["mm"][:, :, 0, :]
    return x2.reshape(B, T, D), tuple(st[n] for n in _STATE_NAMES)


def kernel(x_prompt, x_sample, cache_conv, cache_shift, state_wkv, cache_mconv, state_mC, state_mn, state_mm, norm_ff, ffn_gate, ffn_up, ffn_down, norm_mix, w_in, w_out, conv_w, rwkv_mu, rwkv_w0, rwkv_w_up, rwkv_a0, rwkv_a_up, rwkv_g_up, rwkv_k_k, rwkv_k_a, rwkv_r_k, rwkv_ln_w, rwkv_ln_b, mlstm_conv_w, mlstm_conv_b, mlstm_wq, mlstm_wk, mlstm_wv, mlstm_w_gate, mlstm_b_gate, mlstm_gn_w, mlstm_skip, norm_final):
    W = _prep_weights(norm_ff, ffn_gate, ffn_up, ffn_down, norm_mix, w_in, w_out, conv_w,
                      rwkv_mu, rwkv_w0, rwkv_w_up, rwkv_a0, rwkv_a_up, rwkv_g_up, rwkv_k_k, rwkv_k_a, rwkv_r_k,
                      rwkv_ln_w, rwkv_ln_b, mlstm_conv_w, mlstm_conv_b, mlstm_wq, mlstm_wk, mlstm_wv,
                      mlstm_w_gate, mlstm_b_gate, mlstm_gn_w, mlstm_skip, norm_final)
    depth = w_in.shape[0]
    Bp = x_prompt.shape[0]
    zeros = lambda *shape: jnp.zeros((depth, Bp) + shape, F32)
    st_p = (zeros(CONV_W - 1, D_CONV), zeros(1, P_RWKV), zeros(H_R, HD_R, HD_R), zeros(MCONV_W - 1, D_MLSTM),
            zeros(H_M, HD_M, HD_M), zeros(H_M, HD_M), zeros(H_M))
    y_prompt, p_states = _run_trunk(x_prompt, st_p, W)

    st_s = (cache_conv, cache_shift, state_wkv, cache_mconv, state_mC, state_mn, state_mm)
    y_sample, s_states = _run_trunk(x_sample, st_s, W)
    return (y_prompt, y_sample) + p_states + s_states
```

```python
import functools
import math

import jax
import jax.numpy as jnp
from jax import lax
from jax.experimental import pallas as pl
from jax.experimental.pallas import tpu as pltpu

F32 = jnp.float32
BF16 = jnp.bfloat16

LANES = 128
H_R, HD_R = 6, 64
D_RWKV = H_R * HD_R
N_PAIR = H_R // 2
LORA_W, LORA_A, LORA_G = 32, 32, 64
P_RWKV = 3 * D_RWKV + LORA_W + LORA_A + LORA_G
H_M, HD_M = 4, 96
D_MLSTM = H_M * HD_M
D_MPAD = H_M * LANES
D_CONV = 256
CONV_W, MCONV_W = 3, 4
RMS_EPS = 1e-6
GN_EPS_R = 64e-5
GN_EPS_M = 1e-6
SHORT_SEQ_ROWS = 8
RWKV_CHUNK = 64
RWKV_STEP_ROWS = 256
MLSTM_CHUNK = 256
MLSTM_STEP_ROWS = 256
MLSTM_ROW_BLOCK = 128
RWKV_MAX_SEQS = 16
MLSTM_MAX_SEQS = 8
TOKEN_TILE = 512
FFN_COLS = 256
VMEM_LIMIT = 48 * 1024 * 1024

NN = ((1,), (0,))
NT = ((1,), (1,))
TN = ((0,), (0,))

MODE_PROJ = "b1"
MODE_SCAN = "b1"


def _mm(a, b, dims=NN, mode="b1"):
    dn = (dims, ((), ()))
    dot = lambda x, y: lax.dot_general(x, y, dn, preferred_element_type=F32)
    ah = a.astype(BF16)
    bh = b.astype(BF16)
    if mode == "b1":
        return dot(ah, bh)
    al = (a - ah.astype(F32)).astype(BF16)
    bl = (b - bh.astype(F32)).astype(BF16)
    return dot(ah, bh) + (dot(ah, bl) + dot(al, bh))


def _rms(x, g):
    return x * lax.rsqrt(jnp.mean(x * x, axis=-1, keepdims=True) + RMS_EPS) * g


def _sigmoid(x):
    return 1.0 / (1.0 + jnp.exp(-x))


def _softplus(x):
    return jnp.maximum(x, 0.0) + jnp.log1p(jnp.exp(-jnp.abs(x)))


def _masked_sum(mask, x):
    m = mask.astype(BF16)
    x1 = x.astype(BF16)
    r1 = x - x1.astype(F32)
    x2 = r1.astype(BF16)
    x3 = (r1 - x2.astype(F32)).astype(BF16)
    dot = lambda y: jnp.dot(m, y, preferred_element_type=F32)
    return dot(x1) + (dot(x2) + dot(x3))


def _seq_rows(ref, g, L, tb, packed):
    x = ref[0, g * tb:(g + 1) * tb, :] if packed else ref[g]
    if tb < L:
        x = jnp.concatenate([x, jnp.zeros((L - tb, x.shape[1]), x.dtype)], axis=0)
    return x


def _store_seq(ref, g, tb, packed, rows):
    if packed:
        ref[0, g * tb:(g + 1) * tb, :] = rows[:tb]
    else:
        ref[g] = rows[:tb]


def _cat_rows(xs):
    return xs[0] if len(xs) == 1 else jnp.concatenate(xs, axis=0)


def _half_ffn(x, g_ref, wg_ref, wu_ref, wd_ref):
    n = _rms(x, g_ref[...]).astype(BF16)
    acc = jnp.zeros(x.shape, F32)
    for c in range(wg_ref.shape[1] // FFN_COLS):
        sl = slice(c * FFN_COLS, (c + 1) * FFN_COLS)
        hg = jnp.dot(n, wg_ref[:, sl], preferred_element_type=F32)
        hu = jnp.dot(n, wu_ref[:, sl], preferred_element_type=F32)
        h = (hg * _sigmoid(hg) * hu).astype(BF16)
        acc = acc + jnp.dot(h, wd_ref[sl, :], preferred_element_type=F32)
    return x + 0.5 * acc


def _ffn_body(x_ref, g_ref, wg_ref, wu_ref, wd_ref, o_ref):
    o_ref[...] = _half_ffn(x_ref[...], g_ref, wg_ref, wu_ref, wd_ref)


def _mix_ffn_body(x_ref, yc_ref, yr_ref, ym_ref, wc_ref, wr_ref, wm_ref, g_ref, wg_ref, wu_ref, wd_ref,
                  gf_ref, o_ref, *, final):
    acc = jnp.dot(yc_ref[...].astype(BF16), wc_ref[...], preferred_element_type=F32)
    acc = acc + jnp.dot(yr_ref[...].astype(BF16), wr_ref[...], preferred_element_type=F32)
    acc = acc + jnp.dot(ym_ref[...].astype(BF16), wm_ref[...], preferred_element_type=F32)
    x = _half_ffn(x_ref[...] + acc, g_ref, wg_ref, wu_ref, wd_ref)
    o_ref[...] = _rms(x, gf_ref[...]) if final else x


def _inproj_body(x_ref, g_ref, wc_ref, wr_ref, wm_ref, pc_ref, pr_ref, pm_ref):
    n = _rms(x_ref[...], g_ref[...]).astype(BF16)
    pc_ref[...] = jnp.dot(n, wc_ref[...], preferred_element_type=F32)
    pr_ref[...] = jnp.dot(n, wr_ref[...], preferred_element_type=F32)
    pm_ref[...] = jnp.dot(n, wm_ref[...], preferred_element_type=F32)


def _rows(tm, width):
    return pl.BlockSpec((tm, width), lambda i: (i, 0))


def _resident(shape, *lead):
    nlead = len(lead)
    zeros = (0,) * len(shape)
    return pl.BlockSpec((None,) * nlead + tuple(shape), lambda i: tuple(lead) + zeros,
                        pipeline_mode=pl.Buffered(1))


def _token_params():
    return pltpu.CompilerParams(dimension_semantics=("parallel",), vmem_limit_bytes=VMEM_LIMIT)


def _ffn(x2, W, l, which):
    n, d = x2.shape
    f = W["ffn_gate"].shape[-1]
    tm = min(TOKEN_TILE, n)
    return pl.pallas_call(
        _ffn_body,
        grid=(n // tm,),
        in_specs=[_rows(tm, d), _resident((1, d), l, which), _resident((d, f), l, which),
                  _resident((d, f), l, which), _resident((f, d), l, which)],
        out_specs=_rows(tm, d),
        out_shape=jax.ShapeDtypeStruct((n, d), F32),
        compiler_params=_token_params(),
        name="ffn",
    )(x2, W["norm_ff"], W["ffn_gate"], W["ffn_up"], W["ffn_down"])


def _inproj(x2, W, l):
    n, d = x2.shape
    tm = min(TOKEN_TILE, n)
    wc, wr, wm = W["w_in_c"], W["w_in_r"], W["w_in_m"]
    widths = (wc.shape[-1], wr.shape[-1], wm.shape[-1])
    return pl.pallas_call(
        _inproj_body,
        grid=(n // tm,),
        in_specs=[_rows(tm, d), _resident((1, d), l)] + [_resident((d, w), l) for w in widths],
        out_specs=[_rows(tm, w) for w in widths],
        out_shape=[jax.ShapeDtypeStruct((n, w), F32) for w in widths],
        compiler_params=_token_params(),
        name="inproj",
    )(x2, W["norm_mix"], wc, wr, wm)


def _mix_ffn(x2, yc, yr, ym, W, l, final):
    n, d = x2.shape
    f = W["ffn_gate"].shape[-1]
    tm = min(TOKEN_TILE, n)
    ws = (W["w_out_c"], W["w_out_r"], W["w_out_m"])
    return pl.pallas_call(
        functools.partial(_mix_ffn_body, final=final),
        grid=(n // tm,),
        in_specs=[_rows(tm, d)] + [_rows(tm, w.shape[1]) for w in ws] + [_resident(w.shape[1:], l) for w in ws]
                 + [_resident((1, d), l, 1), _resident((d, f), l, 1), _resident((d, f), l, 1),
                    _resident((f, d), l, 1), _resident((1, d))],
        out_specs=_rows(tm, d),
        out_shape=jax.ShapeDtypeStruct((n, d), F32),
        compiler_params=_token_params(),
        name="mix_ffn",
    )(x2, yc, yr, ym, *ws, W["norm_ff"], W["ffn_gate"], W["ffn_up"], W["ffn_down"], W["norm_final"])


def _short_conv(pc_ref, buf_ref, w_ref, y_ref, nb_ref, xp_scr, c, *, G, L, lv, packed):
    hist = CONV_W - 1
    w = w_ref[...]
    for g in range(G):
        @pl.when(c == 0)
        def _():
            xp_scr[g, 8 - hist:8, :] = buf_ref[g]

        p = _seq_rows(pc_ref, g, L, lv, packed)
        b_gate = p[:, :D_CONV]
        xp_scr[g, 8:8 + L, :] = p[:, D_CONV:2 * D_CONV] * p[:, 2 * D_CONV:]
        z = w[0:1] * xp_scr[g, 6:6 + L, :]
        z = z + w[1:2] * xp_scr[g, 7:7 + L, :]
        z = z + w[2:3] * xp_scr[g, 8:8 + L, :]
        _store_seq(y_ref, g, lv, packed, b_gate * z)

        @pl.when(c == pl.num_programs(1) - 1)
        def _():
            nb_ref[g] = xp_scr[g, 8 + lv - hist:8 + lv, :]

        tail = xp_scr[g, 8 + L - hist:8 + L, :]
        xp_scr[g, 8 - hist:8, :] = tail


def _rwkv_body(pr_ref, sh_ref, s0_ref, mu_ref, w0_ref, wup_ref, a0_ref, aup_ref, gup_ref, kk_ref, ka_ref,
               rk_ref, lnw_ref, lnb_ref, y_ref, sho_ref, so_ref, xp_scr, s_scr, *, G, L, lv, packed):
    c = pl.program_id(1)
    GL = G * L
    shift = L.bit_length() - 1
    seqs = range(G)

    @pl.when(c == 0)
    def _():
        for g in seqs:
            xp_scr[g, 7:8, :] = sh_ref[g]
            for j in range(N_PAIR):
                s_scr[g, j] = jnp.zeros((LANES, LANES), F32)
                s_scr[g, j, :HD_R, :HD_R] = s0_ref[g, 2 * j]
                s_scr[g, j, HD_R:, HD_R:] = s0_ref[g, 2 * j + 1]

    p_seq = [_seq_rows(pr_ref, g, L, lv, packed) for g in seqs]
    for g in seqs:
        xp_scr[g, 8:8 + L, :] = p_seq[g]
    p = _cat_rows(p_seq)
    prev = _cat_rows([xp_scr[g, 7:7 + L, :] for g in seqs])
    ps = p + (prev - p) * mu_ref[...]
    r = ps[:, 0:D_RWKV]
    k = ps[:, D_RWKV:2 * D_RWKV]
    v = ps[:, 2 * D_RWKV:3 * D_RWKV]
    x4 = ps[:, 3 * D_RWKV:]
    lw = -math.exp(-0.5) * _sigmoid(w0_ref[...] + _mm(jnp.tanh(x4), wup_ref[...], NN, MODE_PROJ))
    a = _sigmoid(a0_ref[...] + _mm(x4, aup_ref[...], NN, MODE_PROJ))
    gate = _mm(_sigmoid(x4), gup_ref[...], NN, MODE_PROJ)

    lane = lax.broadcasted_iota(jnp.int32, (1, LANES), 1)
    lo = lane < HD_R

    def pair_sum(x):
        s0 = jnp.sum(jnp.where(lo, x, 0.0), axis=-1, keepdims=True)
        s1 = jnp.sum(jnp.where(lo, 0.0, x), axis=-1, keepdims=True)
        return jnp.where(lo, s0, s1)

    def head_sum(x):
        return jnp.concatenate([pair_sum(x[:, j * LANES:(j + 1) * LANES]) for j in range(N_PAIR)], axis=1)

    kk = k * kk_ref[...]
    kk = kk * lax.rsqrt(jnp.maximum(head_sum(kk * kk), 1e-24))
    k2 = k * (1.0 + (a - 1.0) * ka_ref[...])
    bonus = head_sum(r * k2 * rk_ref[...]) * v

    if lv < L:
        valid = (lax.broadcasted_iota(jnp.int32, (GL, 1), 0) & (L - 1)) < lv
        lw = jnp.where(valid, lw, 0.0)
        kk = jnp.where(valid, kk, 0.0)
        k2 = jnp.where(valid, k2, 0.0)

    ri = lax.broadcasted_iota(jnp.int32, (GL, GL), 0)
    ci = lax.broadcasted_iota(jnp.int32, (GL, GL), 1)
    tri = jnp.logical_and(ci <= ri, (ci >> shift) == (ri >> shift))
    cl = _masked_sum(tri, lw)
    e_pos = jnp.exp(cl)
    e_neg = jnp.exp(-cl)
    a_t = -kk * jnp.exp(cl - lw)
    b_t = kk * a * e_neg
    k_t = k2 * e_neg
    r_t = r * e_pos

    L2 = 2 * L
    ri2 = lax.broadcasted_iota(jnp.int32, (L2, L2), 0)
    ci2 = lax.broadcasted_iota(jnp.int32, (L2, L2), 1)
    tri2 = ri2 & (L - 1)
    tci2 = ci2 & (L - 1)
    strict = tci2 < tri2
    incl = tci2 <= tri2
    eye = (ri2 == ci2).astype(F32)
    row_lo = lax.broadcasted_iota(jnp.int32, (L2, 1), 0) < L
    own = jnp.logical_and(row_lo, lo) | jnp.logical_and(jnp.logical_not(row_lo), jnp.logical_not(lo))

    def stack(x):
        return jnp.concatenate([jnp.where(lo, x, 0.0), jnp.where(lo, 0.0, x)], axis=0)

    units = [(g, j) for g in seqs for j in range(N_PAIR)]
    U = range(len(units))
    rsl = [slice(g * L, (g + 1) * L) for g, _ in units]
    lsl = [slice(j * LANES, (j + 1) * LANES) for _, j in units]
    mm = functools.partial(_mm, mode=MODE_SCAN)
    xa, xb, xk, xr, xv = ([stack(t[rsl[i], lsl[i]]) for i in U] for t in (a_t, b_t, k_t, r_t, v))
    s_bd = [s_scr[g, j] for g, j in units]
    xar = [jnp.concatenate([xa[i], xr[i]], axis=0) for i in U]
    gram = [mm(xar[i], jnp.concatenate([xb[i], xk[i]], axis=0), NT) for i in U]
    a_ab = [jnp.where(strict, gm[:L2, :L2], 0.0) for gm in gram]
    a_ak = [jnp.where(strict, gm[:L2, L2:], 0.0) for gm in gram]
    m_rb = [jnp.where(incl, gm[L2:, :L2], 0.0) for gm in gram]
    m_rk = [jnp.where(incl, gm[L2:, L2:], 0.0) for gm in gram]
    sh = [mm(xar[i], s_bd[i], NT) for i in U]
    akv = [sh[i] + mm(jnp.concatenate([a_ak[i], m_rk[i]], axis=0), xv[i], NN) for i in U]
    tinv = [eye + m for m in a_ab]
    steps = L.bit_length() - 2
    if steps > 0:
        pw = [mm(m, m, NN) for m in a_ab]
    for step in range(steps):
        if step < steps - 1:
            both = [mm(pw[i], jnp.concatenate([tinv[i], pw[i]], axis=1), NN) for i in U]
            tinv = [tinv[i] + both[i][:, :L2] for i in U]
            pw = [both[i][:, L2:] for i in U]
        else:
            tinv = [tinv[i] + mm(pw[i], tinv[i], NN) for i in U]
    u = [mm(tinv[i], akv[i][:L2], NN) for i in U]
    o = [akv[i][L2:] + mm(m_rb[i], u[i], NN) for i in U]
    for i, (g, j) in enumerate(units):
        g_last = e_pos[g * L + L - 1:g * L + L, lsl[i]]
        upd = mm(jnp.concatenate([u[i], xv[i]], axis=0), jnp.concatenate([xb[i], xk[i]], axis=0), TN)
        s_scr[g, j] = (s_bd[i] + upd) * g_last
    outs = []
    for i in U:
        mean = jnp.sum(o[i], axis=-1, keepdims=True) * (1.0 / HD_R)
        oc = jnp.where(own, o[i] - mean, 0.0)
        var = jnp.sum(oc * oc, axis=-1, keepdims=True) * (1.0 / HD_R)
        on = oc * lax.rsqrt(var + GN_EPS_R)
        outs.append(on[:L] + on[L:])

    yn = _cat_rows([jnp.concatenate(outs[g * N_PAIR:(g + 1) * N_PAIR], axis=1) for g in seqs])
    y = (yn * lnw_ref[...] + lnb_ref[...] + bonus) * gate
    for g in seqs:
        _store_seq(y_ref, g, lv, packed, y[g * L:(g + 1) * L])

    @pl.when(c == pl.num_programs(1) - 1)
    def _():
        for g in seqs:
            sho_ref[g] = xp_scr[g, 8 + lv - 1:8 + lv, :]
            for j in range(N_PAIR):
                so_ref[g, 2 * j] = s_scr[g, j, :HD_R, :HD_R]
                so_ref[g, 2 * j + 1] = s_scr[g, j, HD_R:, HD_R:]

    for g in seqs:
        xp_scr[g, 7:8, :] = xp_scr[g, 7 + L:8 + L, :]


def _mlstm_body(pm_ref, pc_ref, cb_ref, c0_ref, n0_ref, m0_ref, sc_ref, cw_ref, cbias_ref, wq_ref, wk_ref, wv_ref,
                wgq_ref, wgk_ref, wgv_ref, bg_ref, gnw_ref, skip_ref, scw_ref,
                y_ref, yc_ref, cbo_ref, co_ref, no_ref, mo_ref, sco_ref, xp_scr, c_scr, m_scr, xpc_scr,
                *, G, L, lv, packed):
    c = pl.program_id(1)
    _short_conv(pc_ref, sc_ref, scw_ref, yc_ref, sco_ref, xpc_scr, c, G=G, L=L, lv=lv, packed=packed)
    hist = MCONV_W - 1
    GL = G * L
    shift = L.bit_length() - 1
    seqs = range(G)

    @pl.when(c == 0)
    def _():
        for g in seqs:
            xp_scr[g, 0:8, :] = jnp.zeros((8, D_MPAD), F32)
            cb = cb_ref[g]
            m0 = m0_ref[g]
            n0 = n0_ref[g]
            for h in range(H_M):
                xp_scr[g, 8 - hist:8, h * LANES:h * LANES + HD_M] = cb[:, h * HD_M:(h + 1) * HD_M]
                c_scr[g, h] = jnp.zeros((LANES, LANES), F32)
                c_scr[g, h, :HD_M, :HD_M] = c0_ref[g, h]
                c_scr[g, h, HD_M:HD_M + 1, :HD_M] = n0[h:h + 1, :]
                m_scr[g, h:h + 1, :] = jnp.broadcast_to(m0[:, h:h + 1], (1, LANES))

    cw = cw_ref[...]
    pm_seq = [_seq_rows(pm_ref, g, L, lv, packed) for g in seqs]
    ucs = []
    for g in seqs:
        u_g = pm_seq[g][:, :D_MPAD]
        xp_scr[g, 8:8 + L, :] = u_g
        acc = cw[0:1] * xp_scr[g, 5:5 + L, :]
        acc = acc + cw[1:2] * xp_scr[g, 6:6 + L, :]
        acc = acc + cw[2:3] * xp_scr[g, 7:7 + L, :]
        ucs.append(acc + cw[3:4] * u_g)
    pm = _cat_rows(pm_seq)
    u = pm[:, :D_MPAD]
    og = pm[:, D_MPAD:]
    uc = _cat_rows(ucs) + cbias_ref[...]
    uc = uc * _sigmoid(uc)
    q = _mm(uc, wq_ref[...], NN, MODE_PROJ)
    k = _mm(uc, wk_ref[...], NN, MODE_PROJ)
    v = _mm(u, wv_ref[...], NN, MODE_PROJ)
    gp = (_mm(q, wgq_ref[...], NN, MODE_PROJ) + _mm(k, wgk_ref[...], NN, MODE_PROJ)
          + _mm(v, wgv_ref[...], NN, MODE_PROJ)) + bg_ref[...]
    k = k * (HD_M ** -0.5)

    ig = gp
    lf = -_softplus(-gp)
    if lv < L:
        valid = (lax.broadcasted_iota(jnp.int32, (GL, 1), 0) & (L - 1)) < lv
        ig = jnp.where(valid, ig, -jnp.inf)
        lf = jnp.where(valid, lf, 0.0)
    rg = lax.broadcasted_iota(jnp.int32, (GL, GL), 0)
    cg = lax.broadcasted_iota(jnp.int32, (GL, GL), 1)
    tri = jnp.logical_and(cg <= rg, (cg >> shift) == (rg >> shift))
    bcum = _masked_sum(tri, lf)
    lane = lax.broadcasted_iota(jnp.int32, (1, LANES), 1)
    gates = jnp.where(lane < H_M, ig, bcum)
    gates_t = gates.T
    head_lane = lane < HD_M
    one_lane = lane == HD_M
    ri = lax.broadcasted_iota(jnp.int32, (L, L), 0)
    ci = lax.broadcasted_iota(jnp.int32, (L, L), 1)
    causal = ci <= ri

    units = [(g, h) for g in seqs for h in range(H_M)]
    U = range(len(units))
    mm = functools.partial(_mm, mode=MODE_SCAN)
    rsl = [slice(g * L, (g + 1) * L) for g, _ in units]
    lsl = [slice(h * LANES, (h + 1) * LANES) for _, h in units]
    qh = [q[rsl[i], lsl[i]] for i in U]
    kh = [k[rsl[i], lsl[i]] for i in U]
    vh = [jnp.where(one_lane, 1.0, v[rsl[i], lsl[i]]) for i in U]
    i_c = [gates[rsl[i], h:h + 1] for i, (g, h) in enumerate(units)]
    b_c = [gates[rsl[i], H_M + h:H_M + h + 1] for i, (g, h) in enumerate(units)]
    i_r = [gates_t[h:h + 1, rsl[i]] for i, (g, h) in enumerate(units)]
    b_r = [gates_t[H_M + h:H_M + h + 1, rsl[i]] for i, (g, h) in enumerate(units)]
    m_prev = [m_scr[g, h:h + 1, 0:1] for g, h in units]
    c_h = [c_scr[g, h] for g, h in units]
    qc = [mm(qh[i], c_h[i], NT) for i in U]
    rowv = [i_r[i] - b_r[i] for i in U]
    inter = [b_c[i] + m_prev[i] for i in U]
    RB = min(L, MLSTM_ROW_BLOCK)
    hh = [[] for _ in U]
    m_last = [None for _ in U]
    for rb in range(L // RB):
        r0, r1 = rb * RB, (rb + 1) * RB
        cmask = causal[r0:r1, :r1]
        dm = [jnp.where(cmask, b_c[i][r0:r1] + rowv[i][:, :r1], -jnp.inf) for i in U]
        m_t = [jnp.maximum(inter[i][r0:r1], jnp.max(dm[i], axis=-1, keepdims=True)) for i in U]
        qk = [mm(qh[i][r0:r1], kh[i][:r1], NT) for i in U]
        s = [qk[i] * jnp.exp(dm[i] - m_t[i]) for i in U]
        w_i = [jnp.exp(inter[i][r0:r1] - m_t[i]) for i in U]
        num = [mm(s[i], vh[i][:r1], NN) + w_i[i] * qc[i][r0:r1] for i in U]
        for i in U:
            den = num[i][:, HD_M:HD_M + 1]
            hh[i].append(jnp.where(head_lane, num[i], 0.0) / jnp.maximum(jnp.abs(den), jnp.exp(-m_t[i])))
            m_last[i] = m_t[i][RB - 1:RB]
    hh = [_cat_rows(h) for h in hh]
    outs = []
    for i, (g, h) in enumerate(units):
        b_l = b_c[i][L - 1:L]
        m_new = m_last[i]
        w_s = jnp.exp(b_l - b_c[i] + i_c[i] - m_new)
        w_p = jnp.exp(b_l + m_prev[i] - m_new)
        ks = w_s * kh[i]
        c_scr[g, h] = w_p * c_h[i] + mm(vh[i], ks, TN)
        m_scr[g, h:h + 1, :] = jnp.broadcast_to(m_new, (1, LANES))
        mean = jnp.sum(hh[i], axis=-1, keepdims=True) * (1.0 / HD_M)
        hc = jnp.where(head_lane, hh[i] - mean, 0.0)
        var = jnp.sum(hc * hc, axis=-1, keepdims=True) * (1.0 / HD_M)
        outs.append(hc * lax.rsqrt(var + GN_EPS_M))

    hn = _cat_rows([jnp.concatenate(outs[g * H_M:(g + 1) * H_M], axis=1) for g in seqs])
    y = (hn * gnw_ref[...] + skip_ref[...] * uc) * _sigmoid(og)
    for g in seqs:
        _store_seq(y_ref, g, lv, packed, y[g * L:(g + 1) * L])

    @pl.when(c == pl.num_programs(1) - 1)
    def _():
        lane_h = lax.broadcasted_iota(jnp.int32, (1, H_M), 1)
        for g in seqs:
            rows = xp_scr[g, 8 + lv - hist:8 + lv, :]
            cbo_ref[g] = jnp.concatenate([rows[:, h * LANES:h * LANES + HD_M] for h in range(H_M)], axis=1)
            no_ref[g] = jnp.concatenate([c_scr[g, h, HD_M:HD_M + 1, :HD_M] for h in range(H_M)], axis=0)
            m_row = jnp.zeros((1, H_M), F32)
            for h in range(H_M):
                co_ref[g, h] = c_scr[g, h, :HD_M, :HD_M]
                m_row = jnp.where(lane_h == h, m_scr[g, h:h + 1, 0:1], m_row)
            mo_ref[g] = m_row

    for g in seqs:
        tail = xp_scr[g, 8 + L - hist:8 + L, :]
        xp_scr[g, 8 - hist:8, :] = tail


def _state_spec(G, shape, l):
    zeros = (0,) * len(shape)
    return pl.BlockSpec((None, G) + tuple(shape), lambda b, c: (l, b) + zeros)


def _param_spec(shape, l):
    zeros = (0,) * len(shape)
    return pl.BlockSpec((None,) + tuple(shape), lambda b, c: (l,) + zeros)


def _mixer_params():
    return pltpu.CompilerParams(dimension_semantics=("parallel", "arbitrary"), vmem_limit_bytes=VMEM_LIMIT)


class _SeqLayout:
    def __init__(self, B, T, chunk, step_rows, max_seqs):
        self.B, self.T = B, T
        self.packed = T < SHORT_SEQ_ROWS
        if self.packed:
            self.tb, self.L, self.nc = T, SHORT_SEQ_ROWS, 1
        else:
            self.tb = self.L = min(T, chunk)
            self.nc = T // self.L
            assert T % self.L == 0
        assert self.L & (self.L - 1) == 0
        self.G = max(1, min(B, step_rows // self.L, max_seqs))
        assert B % self.G == 0
        self.grid = (B // self.G, self.nc)
        self.static = dict(G=self.G, L=self.L, lv=self.tb, packed=self.packed)

    def view(self, a):
        shape = (self.B // self.G, self.G * self.T) if self.packed else (self.B, self.T)
        return a.reshape(shape + (a.shape[-1],))

    def spec(self, width):
        if self.packed:
            return pl.BlockSpec((1, self.G * self.T, width), lambda b, c: (b, 0, 0))
        return pl.BlockSpec((self.G, self.tb, width), lambda b, c: (b, c, 0))

    def out_shape(self, width):
        shape = (self.B // self.G, self.G * self.T) if self.packed else (self.B, self.T)
        return jax.ShapeDtypeStruct(shape + (width,), F32)


def _rwkv_mixer(pr, st, W, l, B, T):
    lay = _SeqLayout(B, T, RWKV_CHUNK, RWKV_STEP_ROWS, RWKV_MAX_SEQS)
    G, L = lay.G, lay.L
    vec = lambda n: _param_spec((1, n), l)
    shift = _state_spec(G, (1, P_RWKV), l)
    wkv = _state_spec(G, (H_R, HD_R, HD_R), l)
    y, st_shift, st_wkv = pl.pallas_call(
        functools.partial(_rwkv_body, **lay.static),
        grid=lay.grid,
        in_specs=[lay.spec(P_RWKV), shift, wkv,
                  vec(P_RWKV), vec(D_RWKV), _param_spec((LANES, D_RWKV), l), vec(D_RWKV),
                  _param_spec((LANES, D_RWKV), l), _param_spec((LANES, D_RWKV), l),
                  vec(D_RWKV), vec(D_RWKV), vec(D_RWKV), vec(D_RWKV), vec(D_RWKV)],
        out_specs=[lay.spec(D_RWKV), shift, wkv],
        out_shape=[lay.out_shape(D_RWKV), jax.ShapeDtypeStruct(st["shift"].shape, F32),
                   jax.ShapeDtypeStruct(st["wkv"].shape, F32)],
        input_output_aliases={1: 1, 2: 2},
        scratch_shapes=[pltpu.VMEM((G, 8 + L, P_RWKV), F32), pltpu.VMEM((G, N_PAIR, LANES, LANES), F32)],
        compiler_params=_mixer_params(),
        name="rwkv7",
    )(lay.view(pr), st["shift"], st["wkv"], W["rwkv_mu"], W["rwkv_w0"], W["rwkv_w_up"], W["rwkv_a0"],
      W["rwkv_a_up"], W["rwkv_g_up"], W["rwkv_k_k"], W["rwkv_k_a"], W["rwkv_r_k"], W["rwkv_ln_w"], W["rwkv_ln_b"])
    return y.reshape(B * T, D_RWKV), st_shift, st_wkv


def _mlstm_conv_mixer(pm, pc, st, W, l, B, T):
    lay = _SeqLayout(B, T, MLSTM_CHUNK, MLSTM_STEP_ROWS, MLSTM_MAX_SEQS)
    G, L = lay.G, lay.L
    hist = MCONV_W - 1
    vec = lambda n: _param_spec((1, n), l)
    sq = _param_spec((D_MPAD, D_MPAD), l)
    gate = _param_spec((D_MPAD, LANES), l)
    states = [_state_spec(G, (hist, D_MLSTM), l), _state_spec(G, (H_M, HD_M, HD_M), l),
              _state_spec(G, (H_M, HD_M), l), _state_spec(G, (1, H_M), l),
              _state_spec(G, (CONV_W - 1, D_CONV), l)]
    names = ("mconv", "mC", "mn", "mm", "conv")
    outs = pl.pallas_call(
        functools.partial(_mlstm_body, **lay.static),
        grid=lay.grid,
        in_specs=[lay.spec(2 * D_MPAD), lay.spec(3 * D_CONV)] + states
                 + [_param_spec((MCONV_W, D_MPAD), l), vec(D_MPAD), sq, sq, sq, gate, gate, gate, vec(LANES),
                    vec(D_MPAD), vec(D_MPAD), _param_spec((CONV_W, D_CONV), l)],
        out_specs=[lay.spec(D_MPAD), lay.spec(D_CONV)] + states,
        out_shape=[lay.out_shape(D_MPAD), lay.out_shape(D_CONV)]
                  + [jax.ShapeDtypeStruct(st[n].shape, F32) for n in names],
        input_output_aliases={2: 2, 3: 3, 4: 4, 5: 5, 6: 6},
        scratch_shapes=[pltpu.VMEM((G, 8 + L, D_MPAD), F32), pltpu.VMEM((G, H_M, LANES, LANES), F32),
                        pltpu.VMEM((G, H_M, LANES), F32), pltpu.VMEM((G, 8 + L, D_CONV), F32)],
        compiler_params=_mixer_params(),
        name="mlstm_conv",
    )(lay.view(pm), lay.view(pc), *[st[n] for n in names], W["mlstm_conv_w"], W["mlstm_conv_b"], W["mlstm_wq"],
      W["mlstm_wk"], W["mlstm_wv"], W["mlstm_wgq"], W["mlstm_wgk"], W["mlstm_wgv"], W["mlstm_b_gate"],
      W["mlstm_gn_w"], W["mlstm_skip"], W["conv_w"])
    return (outs[0].reshape(B * T, D_MPAD), outs[1].reshape(B * T, D_CONV)) + tuple(outs[2:])


def _pad_heads(x, axis):
    axis = axis % x.ndim
    shp = x.shape
    x = x.reshape(shp[:axis] + (H_M, HD_M) + shp[axis + 1:])
    pad = [(0, 0)] * x.ndim
    pad[axis + 1] = (0, LANES - HD_M)
    x = jnp.pad(x, pad)
    return x.reshape(shp[:axis] + (D_MPAD,) + shp[axis + 1:])


def _head_block_diag(w):
    wp = jnp.pad(w, ((0, 0), (0, 0), (0, LANES - HD_M), (0, LANES - HD_M)))
    eye = jnp.eye(H_M, dtype=w.dtype)
    return jnp.einsum("lhde,hg->lhdge", wp, eye).reshape(w.shape[0], D_MPAD, D_MPAD)


def _prep_weights(norm_ff, ffn_gate, ffn_up, ffn_down, norm_mix, w_in, w_out, conv_w,
                  rwkv_mu, rwkv_w0, rwkv_w_up, rwkv_a0, rwkv_a_up, rwkv_g_up, rwkv_k_k, rwkv_k_a, rwkv_r_k,
                  rwkv_ln_w, rwkv_ln_b, mlstm_conv_w, mlstm_conv_b, mlstm_wq, mlstm_wk, mlstm_wv,
                  mlstm_w_gate, mlstm_b_gate, mlstm_gn_w, mlstm_skip, norm_final):
    depth = w_in.shape[0]
    pc, pr = 3 * D_CONV, P_RWKV
    row = lambda x: x[:, None, :]
    W = {}
    W["norm_ff"] = norm_ff[:, :, None, :]
    W["ffn_gate"] = ffn_gate.astype(BF16)
    W["ffn_up"] = ffn_up.astype(BF16)
    W["ffn_down"] = ffn_down.astype(BF16)
    W["norm_mix"] = row(norm_mix)
    W["w_in_c"] = w_in[:, :, :pc].astype(BF16)
    W["w_in_r"] = w_in[:, :, pc:pc + pr].astype(BF16)
    w_u = _pad_heads(w_in[:, :, pc + pr:pc + pr + D_MLSTM], 2)
    w_o = _pad_heads(w_in[:, :, pc + pr + D_MLSTM:], 2)
    W["w_in_m"] = jnp.concatenate([w_u, w_o], axis=2).astype(BF16)
    W["w_out_c"] = w_out[:, :D_CONV].astype(BF16)
    W["w_out_r"] = w_out[:, D_CONV:D_CONV + D_RWKV].astype(BF16)
    W["w_out_m"] = _pad_heads(w_out[:, D_CONV + D_RWKV:], 1).astype(BF16)
    W["conv_w"] = conv_w
    W["rwkv_mu"] = row(rwkv_mu)
    W["rwkv_w0"] = row(rwkv_w0)
    W["rwkv_a0"] = row(rwkv_a0)
    zl = lambda n: jnp.zeros((depth, n, D_RWKV), F32)
    W["rwkv_w_up"] = jnp.concatenate([rwkv_w_up, zl(LANES - LORA_W)], axis=1)
    W["rwkv_a_up"] = jnp.concatenate([zl(LORA_W), rwkv_a_up, zl(LORA_G)], axis=1)
    W["rwkv_g_up"] = jnp.concatenate([zl(LORA_W + LORA_A), rwkv_g_up], axis=1)
    W["rwkv_k_k"] = row(rwkv_k_k)
    W["rwkv_k_a"] = row(rwkv_k_a)
    W["rwkv_r_k"] = row(rwkv_r_k.reshape(depth, D_RWKV))
    W["rwkv_ln_w"] = row(rwkv_ln_w)
    W["rwkv_ln_b"] = row(rwkv_ln_b)
    W["mlstm_conv_w"] = _pad_heads(mlstm_conv_w, 2)
    W["mlstm_conv_b"] = row(_pad_heads(mlstm_conv_b, 1))
    W["mlstm_wq"] = _head_block_diag(mlstm_wq)
    W["mlstm_wk"] = _head_block_diag(mlstm_wk)
    W["mlstm_wv"] = _head_block_diag(mlstm_wv)
    wg = mlstm_w_gate.reshape(depth, H_M, 3, HD_M, 2 * H_M)
    for i, name in enumerate(("mlstm_wgq", "mlstm_wgk", "mlstm_wgv")):
        part = jnp.pad(wg[:, :, i], ((0, 0), (0, 0), (0, LANES - HD_M), (0, LANES - 2 * H_M)))
        W[name] = part.reshape(depth, D_MPAD, LANES)
    W["mlstm_b_gate"] = row(jnp.pad(mlstm_b_gate, ((0, 0), (0, LANES - 2 * H_M))))
    W["mlstm_gn_w"] = row(_pad_heads(mlstm_gn_w, 1))
    W["mlstm_skip"] = row(_pad_heads(mlstm_skip, 1))
    W["norm_final"] = norm_final[None, :]
    return W


_STATE_NAMES = ("conv", "shift", "wkv", "mconv", "mC", "mn", "mm")


def _run_trunk(x, states, W):
    B, T, D = x.shape
    depth = W["w_in_c"].shape[0]
    st = dict(zip(_STATE_NAMES, states))
    st["mm"] = st["mm"][:, :, None, :]
    x2 = x.reshape(B * T, D)
    for l in range(depth):
        x2 = _ffn(x2, W, l, 0)
        pc, pr, pm = _inproj(x2, W, l)
        yr, st["shift"], st["wkv"] = _rwkv_mixer(pr, st, W, l, B, T)
        ym, yc, st["mconv"], st["mC"], st["mn"], st["mm"], st["conv"] = _mlstm_conv_mixer(pm, pc, st, W, l, B, T)
        x2 = _mix_ffn(x2, yc, yr, ym, W, l, final=(l == depth - 1))
    st["mm"] = st["mm"][:, :, 0, :]
    return x2.reshape(B, T, D), tuple(st[n] for n in _STATE_NAMES)


def kernel(x_prompt, x_sample, cache_conv, cache_shift, state_wkv, cache_mconv, state_mC, state_mn, state_mm, norm_ff, ffn_gate, ffn_up, ffn_down, norm_mix, w_in, w_out, conv_w, rwkv_mu, rwkv_w0, rwkv_w_up, rwkv_a0, rwkv_a_up, rwkv_g_up, rwkv_k_k, rwkv_k_a, rwkv_r_k, rwkv_ln_w, rwkv_ln_b, mlstm_conv_w, mlstm_conv_b, mlstm_wq, mlstm_wk, mlstm_wv, mlstm_w_gate, mlstm_b_gate, mlstm_gn_w, mlstm_skip, norm_final):
    W = _prep_weights(norm_ff, ffn_gate, ffn_up, ffn_down, norm_mix, w_in, w_out, conv_w,
                      rwkv_mu, rwkv_w0, rwkv_w_up, rwkv_a0, rwkv_a_up, rwkv_g_up, rwkv_k_k, rwkv_k_a, rwkv_r_k,
                      rwkv_ln_w, rwkv_ln_b, mlstm_conv_w, mlstm_conv_b, mlstm_wq, mlstm_wk, mlstm_wv,
                      mlstm_w_gate, mlstm_b_gate, mlstm_gn_w, mlstm_skip, norm_final)
    depth = w_in.shape[0]
    Bp = x_prompt.shape[0]
    zeros = lambda *shape: jnp.zeros((depth, Bp) + shape, F32)
    st_p = (zeros(CONV_W - 1, D_CONV), zeros(1, P_RWKV), zeros(H_R, HD_R, HD_R), zeros(MCONV_W - 1, D_MLSTM),
            zeros(H_M, HD_M, HD_M), zeros(H_M, HD_M), zeros(H_M))
    y_prompt, p_states = _run_trunk(x_prompt, st_p, W)

    st_s = (cache_conv, cache_shift, state_wkv, cache_mconv, state_mC, state_mn, state_mm)
    y_sample, s_states = _run_trunk(x_sample, st_s, W)
    return (y_prompt, y_sample) + p_states + s_states
```

```python
import functools
import math

import jax
import jax.numpy as jnp
from jax import lax
from jax.experimental import pallas as pl
from jax.experimental.pallas import tpu as pltpu

F32 = jnp.float32
BF16 = jnp.bfloat16

LANES = 128
H_R, HD_R = 6, 64
D_RWKV = H_R * HD_R
N_PAIR = H_R // 2
LORA_W, LORA_A, LORA_G = 32, 32, 64
P_RWKV = 3 * D_RWKV + LORA_W + LORA_A + LORA_G
H_M, HD_M = 4, 96
D_MLSTM = H_M * HD_M
D_MPAD = H_M * LANES
D_CONV = 256
CONV_W, MCONV_W = 3, 4
RMS_EPS = 1e-6
GN_EPS_R = 64e-5
GN_EPS_M = 1e-6
SHORT_SEQ_ROWS = 8
RWKV_CHUNK = 64
RWKV_STEP_ROWS = 256
MLSTM_CHUNK = 256
MLSTM_STEP_ROWS = 256
MLSTM_ROW_BLOCK = 128
RWKV_MAX_SEQS = 16
MLSTM_MAX_SEQS = 8
TOKEN_TILE = 512
FFN_COLS = 256
VMEM_LIMIT = 52 * 1024 * 1024

NN = ((1,), (0,))
NT = ((1,), (1,))
TN = ((0,), (0,))

MODE_PROJ = "b1"
MODE_SCAN = "b1"


def _mm(a, b, dims=NN, mode="b1"):
    dn = (dims, ((), ()))
    dot = lambda x, y: lax.dot_general(x, y, dn, preferred_element_type=F32)
    ah = a.astype(BF16)
    bh = b.astype(BF16)
    if mode == "b1":
        return dot(ah, bh)
    al = (a - ah.astype(F32)).astype(BF16)
    bl = (b - bh.astype(F32)).astype(BF16)
    return dot(ah, bh) + (dot(ah, bl) + dot(al, bh))


def _rms(x, g):
    return x * lax.rsqrt(jnp.mean(x * x, axis=-1, keepdims=True) + RMS_EPS) * g


def _sigmoid(x):
    return 1.0 / (1.0 + jnp.exp(-x))


def _softplus(x):
    return jnp.maximum(x, 0.0) + jnp.log1p(jnp.exp(-jnp.abs(x)))


def _masked_sum(mask, x):
    m = mask.astype(BF16)
    x1 = x.astype(BF16)
    r1 = x - x1.astype(F32)
    x2 = r1.astype(BF16)
    x3 = (r1 - x2.astype(F32)).astype(BF16)
    dot = lambda y: jnp.dot(m, y, preferred_element_type=F32)
    return dot(x1) + (dot(x2) + dot(x3))


def _seq_rows(ref, g, L, tb, packed):
    x = ref[0, g * tb:(g + 1) * tb, :] if packed else ref[g]
    if tb < L:
        x = jnp.concatenate([x, jnp.zeros((L - tb, x.shape[1]), x.dtype)], axis=0)
    return x


def _store_seq(ref, g, tb, packed, rows):
    if packed:
        ref[0, g * tb:(g + 1) * tb, :] = rows[:tb]
    else:
        ref[g] = rows[:tb]


def _cat_rows(xs):
    return xs[0] if len(xs) == 1 else jnp.concatenate(xs, axis=0)


def _half_ffn(x, g_ref, wg_ref, wu_ref, wd_ref):
    n = _rms(x, g_ref[...]).astype(BF16)
    acc = jnp.zeros(x.shape, F32)
    for c in range(wg_ref.shape[1] // FFN_COLS):
        sl = slice(c * FFN_COLS, (c + 1) * FFN_COLS)
        hg = jnp.dot(n, wg_ref[:, sl], preferred_element_type=F32)
        hu = jnp.dot(n, wu_ref[:, sl], preferred_element_type=F32)
        h = (hg * _sigmoid(hg) * hu).astype(BF16)
        acc = acc + jnp.dot(h, wd_ref[sl, :], preferred_element_type=F32)
    return x + 0.5 * acc


def _ffn_body(x_ref, g_ref, wg_ref, wu_ref, wd_ref, o_ref):
    o_ref[...] = _half_ffn(x_ref[...], g_ref, wg_ref, wu_ref, wd_ref)


def _mix_ffn_body(x_ref, yc_ref, yr_ref, ym_ref, wc_ref, wr_ref, wm_ref, g_ref, wg_ref, wu_ref, wd_ref,
                  gf_ref, o_ref, *, final):
    acc = jnp.dot(yc_ref[...].astype(BF16), wc_ref[...], preferred_element_type=F32)
    acc = acc + jnp.dot(yr_ref[...].astype(BF16), wr_ref[...], preferred_element_type=F32)
    acc = acc + jnp.dot(ym_ref[...].astype(BF16), wm_ref[...], preferred_element_type=F32)
    x = _half_ffn(x_ref[...] + acc, g_ref, wg_ref, wu_ref, wd_ref)
    o_ref[...] = _rms(x, gf_ref[...]) if final else x


def _inproj_body(x_ref, g_ref, wc_ref, wr_ref, wm_ref, pc_ref, pr_ref, pm_ref):
    n = _rms(x_ref[...], g_ref[...]).astype(BF16)
    pc_ref[...] = jnp.dot(n, wc_ref[...], preferred_element_type=F32)
    pr_ref[...] = jnp.dot(n, wr_ref[...], preferred_element_type=F32)
    pm_ref[...] = jnp.dot(n, wm_ref[...], preferred_element_type=F32)


def _resident(shape, *lead):
    nlead = len(lead)
    zeros = (0,) * len(shape)
    return pl.BlockSpec((None,) * nlead + tuple(shape), lambda i: tuple(lead) + zeros,
                        pipeline_mode=pl.Buffered(1))


def _two_trunk_body(*refs, inner, n_x, n_par, n_out, tiles):
    xs_p, xs_s = refs[:n_x], refs[n_x:2 * n_x]
    params = refs[2 * n_x:2 * n_x + n_par]
    outs = refs[2 * n_x + n_par:]
    i = pl.program_id(0)

    @pl.when(i < tiles)
    def _():
        inner(*xs_p, *params, *outs[:n_out])

    @pl.when(i == tiles)
    def _():
        inner(*xs_s, *params, *outs[n_out:])


def _token_call(inner, xs_p, xs_s, params, param_specs, out_widths, name):
    n_p, n_s = xs_p[0].shape[0], xs_s[0].shape[0]
    tm = min(TOKEN_TILE, n_p)
    tiles = n_p // tm
    assert n_p % tm == 0
    rows_p = lambda w: pl.BlockSpec((tm, w), lambda i: (jnp.minimum(i, tiles - 1), 0))
    rows_s_in = lambda w: pl.BlockSpec((n_s, w), lambda i: (0, 0), pipeline_mode=pl.Buffered(1))
    rows_s_out = lambda w: pl.BlockSpec((n_s, w), lambda i: (0, 0))
    outs = pl.pallas_call(
        functools.partial(_two_trunk_body, inner=inner, n_x=len(xs_p), n_par=len(params),
                          n_out=len(out_widths), tiles=tiles),
        grid=(tiles + 1,),
        in_specs=[rows_p(a.shape[1]) for a in xs_p] + [rows_s_in(a.shape[1]) for a in xs_s] + list(param_specs),
        out_specs=[rows_p(w) for w in out_widths] + [rows_s_out(w) for w in out_widths],
        out_shape=[jax.ShapeDtypeStruct((n_p, w), F32) for w in out_widths]
                  + [jax.ShapeDtypeStruct((n_s, w), F32) for w in out_widths],
        compiler_params=pltpu.CompilerParams(dimension_semantics=("arbitrary",), vmem_limit_bytes=VMEM_LIMIT),
        name=name,
    )(*xs_p, *xs_s, *params)
    k = len(out_widths)
    return outs[:k], outs[k:]


def _ffn(xp, xs, W, l, which):
    d = xp.shape[1]
    f = W["ffn_gate"].shape[-1]
    specs = [_resident((1, d), l, which), _resident((d, f), l, which), _resident((d, f), l, which),
             _resident((f, d), l, which)]
    (op,), (os_,) = _token_call(_ffn_body, [xp], [xs], [W["norm_ff"], W["ffn_gate"], W["ffn_up"], W["ffn_down"]],
                                specs, [d], "ffn")
    return op, os_


def _inproj(xp, xs, W, l):
    d = xp.shape[1]
    ws = [W["w_in_c"], W["w_in_r"], W["w_in_m"]]
    widths = [w.shape[-1] for w in ws]
    specs = [_resident((1, d), l)] + [_resident((d, w), l) for w in widths]
    return _token_call(_inproj_body, [xp], [xs], [W["norm_mix"]] + ws, specs, widths, "inproj")


def _mix_ffn(xp, yp, xs, ys, W, l, final):
    d = xp.shape[1]
    f = W["ffn_gate"].shape[-1]
    ws = [W["w_out_c"], W["w_out_r"], W["w_out_m"]]
    specs = ([_resident(w.shape[1:], l) for w in ws]
             + [_resident((1, d), l, 1), _resident((d, f), l, 1), _resident((d, f), l, 1),
                _resident((f, d), l, 1), _resident((1, d))])
    params = ws + [W["norm_ff"], W["ffn_gate"], W["ffn_up"], W["ffn_down"], W["norm_final"]]
    (op,), (os_,) = _token_call(functools.partial(_mix_ffn_body, final=final), [xp] + list(yp), [xs] + list(ys),
                                params, specs, [d], "mix_ffn")
    return op, os_


def _short_conv(pc_ref, buf_ref, w_ref, y_ref, nb_ref, xp_scr, c, *, G, L, lv, packed):
    hist = CONV_W - 1
    w = w_ref[...]
    for g in range(G):
        @pl.when(c == 0)
        def _():
            xp_scr[g, 8 - hist:8, :] = buf_ref[g]

        p = _seq_rows(pc_ref, g, L, lv, packed)
        b_gate = p[:, :D_CONV]
        xp_scr[g, 8:8 + L, :] = p[:, D_CONV:2 * D_CONV] * p[:, 2 * D_CONV:]
        z = w[0:1] * xp_scr[g, 6:6 + L, :]
        z = z + w[1:2] * xp_scr[g, 7:7 + L, :]
        z = z + w[2:3] * xp_scr[g, 8:8 + L, :]
        _store_seq(y_ref, g, lv, packed, b_gate * z)

        @pl.when(c == pl.num_programs(1) - 1)
        def _():
            nb_ref[g] = xp_scr[g, 8 + lv - hist:8 + lv, :]

        tail = xp_scr[g, 8 + L - hist:8 + L, :]
        xp_scr[g, 8 - hist:8, :] = tail


def _rwkv_body(pr_ref, sh_ref, s0_ref, mu_ref, w0_ref, wup_ref, a0_ref, aup_ref, gup_ref, kk_ref, ka_ref,
               rk_ref, lnw_ref, lnb_ref, y_ref, sho_ref, so_ref, xp_scr, s_scr, *, G, L, lv, packed):
    c = pl.program_id(1)
    GL = G * L
    shift = L.bit_length() - 1
    seqs = range(G)

    @pl.when(c == 0)
    def _():
        for g in seqs:
            xp_scr[g, 7:8, :] = sh_ref[g]
            for j in range(N_PAIR):
                s_scr[g, j] = jnp.zeros((LANES, LANES), F32)
                s_scr[g, j, :HD_R, :HD_R] = s0_ref[g, 2 * j]
                s_scr[g, j, HD_R:, HD_R:] = s0_ref[g, 2 * j + 1]

    p_seq = [_seq_rows(pr_ref, g, L, lv, packed) for g in seqs]
    for g in seqs:
        xp_scr[g, 8:8 + L, :] = p_seq[g]
    p = _cat_rows(p_seq)
    prev = _cat_rows([xp_scr[g, 7:7 + L, :] for g in seqs])
    ps = p + (prev - p) * mu_ref[...]
    r = ps[:, 0:D_RWKV]
    k = ps[:, D_RWKV:2 * D_RWKV]
    v = ps[:, 2 * D_RWKV:3 * D_RWKV]
    x4 = ps[:, 3 * D_RWKV:]
    lw = -math.exp(-0.5) * _sigmoid(w0_ref[...] + _mm(jnp.tanh(x4), wup_ref[...], NN, MODE_PROJ))
    a = _sigmoid(a0_ref[...] + _mm(x4, aup_ref[...], NN, MODE_PROJ))
    gate = _mm(_sigmoid(x4), gup_ref[...], NN, MODE_PROJ)

    lane = lax.broadcasted_iota(jnp.int32, (1, LANES), 1)
    lo = lane < HD_R

    def pair_sum(x):
        s0 = jnp.sum(jnp.where(lo, x, 0.0), axis=-1, keepdims=True)
        s1 = jnp.sum(jnp.where(lo, 0.0, x), axis=-1, keepdims=True)
        return jnp.where(lo, s0, s1)

    def head_sum(x):
        return jnp.concatenate([pair_sum(x[:, j * LANES:(j + 1) * LANES]) for j in range(N_PAIR)], axis=1)

    kk = k * kk_ref[...]
    kk = kk * lax.rsqrt(jnp.maximum(head_sum(kk * kk), 1e-24))
    k2 = k * (1.0 + (a - 1.0) * ka_ref[...])
    bonus = head_sum(r * k2 * rk_ref[...]) * v

    if lv < L:
        valid = (lax.broadcasted_iota(jnp.int32, (GL, 1), 0) & (L - 1)) < lv
        lw = jnp.where(valid, lw, 0.0)
        kk = jnp.where(valid, kk, 0.0)
        k2 = jnp.where(valid, k2, 0.0)

    ri = lax.broadcasted_iota(jnp.int32, (GL, GL), 0)
    ci = lax.broadcasted_iota(jnp.int32, (GL, GL), 1)
    tri = jnp.logical_and(ci <= ri, (ci >> shift) == (ri >> shift))
    cl = _masked_sum(tri, lw)
    e_pos = jnp.exp(cl)
    e_neg = jnp.exp(-cl)
    a_t = -kk * jnp.exp(cl - lw)
    b_t = kk * a * e_neg
    k_t = k2 * e_neg
    r_t = r * e_pos

    L2 = 2 * L
    ri2 = lax.broadcasted_iota(jnp.int32, (L2, L2), 0)
    ci2 = lax.broadcasted_iota(jnp.int32, (L2, L2), 1)
    tri2 = ri2 & (L - 1)
    tci2 = ci2 & (L - 1)
    strict = tci2 < tri2
    incl = tci2 <= tri2
    eye = (ri2 == ci2).astype(F32)
    row_lo = lax.broadcasted_iota(jnp.int32, (L2, 1), 0) < L
    own = jnp.logical_and(row_lo, lo) | jnp.logical_and(jnp.logical_not(row_lo), jnp.logical_not(lo))

    def stack(x):
        return jnp.concatenate([jnp.where(lo, x, 0.0), jnp.where(lo, 0.0, x)], axis=0)

    units = [(g, j) for g in seqs for j in range(N_PAIR)]
    U = range(len(units))
    rsl = [slice(g * L, (g + 1) * L) for g, _ in units]
    lsl = [slice(j * LANES, (j + 1) * LANES) for _, j in units]
    mm = functools.partial(_mm, mode=MODE_SCAN)
    xa, xb, xk, xr, xv = ([stack(t[rsl[i], lsl[i]]) for i in U] for t in (a_t, b_t, k_t, r_t, v))
    s_bd = [s_scr[g, j] for g, j in units]
    xar = [jnp.concatenate([xa[i], xr[i]], axis=0) for i in U]
    gram = [mm(xar[i], jnp.concatenate([xb[i], xk[i]], axis=0), NT) for i in U]
    a_ab = [jnp.where(strict, gm[:L2, :L2], 0.0) for gm in gram]
    a_ak = [jnp.where(strict, gm[:L2, L2:], 0.0) for gm in gram]
    m_rb = [jnp.where(incl, gm[L2:, :L2], 0.0) for gm in gram]
    m_rk = [jnp.where(incl, gm[L2:, L2:], 0.0) for gm in gram]
    sh = [mm(xar[i], s_bd[i], NT) for i in U]
    akv = [sh[i] + mm(jnp.concatenate([a_ak[i], m_rk[i]], axis=0), xv[i], NN) for i in U]
    tinv = [eye + m for m in a_ab]
    steps = L.bit_length() - 2
    if steps > 0:
        pw = [mm(m, m, NN) for m in a_ab]
    for step in range(steps):
        if step < steps - 1:
            both = [mm(pw[i], jnp.concatenate([tinv[i], pw[i]], axis=1), NN) for i in U]
            tinv = [tinv[i] + both[i][:, :L2] for i in U]
            pw = [both[i][:, L2:] for i in U]
        else:
            tinv = [tinv[i] + mm(pw[i], tinv[i], NN) for i in U]
    u = [mm(tinv[i], akv[i][:L2], NN) for i in U]
    o = [akv[i][L2:] + mm(m_rb[i], u[i], NN) for i in U]
    for i, (g, j) in enumerate(units):
        g_last = e_pos[g * L + L - 1:g * L + L, lsl[i]]
        upd = mm(jnp.concatenate([u[i], xv[i]], axis=0), jnp.concatenate([xb[i], xk[i]], axis=0), TN)
        s_scr[g, j] = (s_bd[i] + upd) * g_last
    outs = []
    for i in U:
        mean = jnp.sum(o[i], axis=-1, keepdims=True) * (1.0 / HD_R)
        oc = jnp.where(own, o[i] - mean, 0.0)
        var = jnp.sum(oc * oc, axis=-1, keepdims=True) * (1.0 / HD_R)
        on = oc * lax.rsqrt(var + GN_EPS_R)
        outs.append(on[:L] + on[L:])

    yn = _cat_rows([jnp.concatenate(outs[g * N_PAIR:(g + 1) * N_PAIR], axis=1) for g in seqs])
    y = (yn * lnw_ref[...] + lnb_ref[...] + bonus) * gate
    for g in seqs:
        _store_seq(y_ref, g, lv, packed, y[g * L:(g + 1) * L])

    @pl.when(c == pl.num_programs(1) - 1)
    def _():
        for g in seqs:
            sho_ref[g] = xp_scr[g, 8 + lv - 1:8 + lv, :]
            for j in range(N_PAIR):
                so_ref[g, 2 * j] = s_scr[g, j, :HD_R, :HD_R]
                so_ref[g, 2 * j + 1] = s_scr[g, j, HD_R:, HD_R:]

    for g in seqs:
        xp_scr[g, 7:8, :] = xp_scr[g, 7 + L:8 + L, :]


def _mlstm_body(pm_ref, pc_ref, cb_ref, c0_ref, n0_ref, m0_ref, sc_ref, cw_ref, cbias_ref, wq_ref, wk_ref, wv_ref,
                wgq_ref, wgk_ref, wgv_ref, bg_ref, gnw_ref, skip_ref, scw_ref,
                y_ref, yc_ref, cbo_ref, co_ref, no_ref, mo_ref, sco_ref, xp_scr, c_scr, m_scr, xpc_scr,
                *, G, L, lv, packed):
    c = pl.program_id(1)
    _short_conv(pc_ref, sc_ref, scw_ref, yc_ref, sco_ref, xpc_scr, c, G=G, L=L, lv=lv, packed=packed)
    hist = MCONV_W - 1
    GL = G * L
    shift = L.bit_length() - 1
    seqs = range(G)

    @pl.when(c == 0)
    def _():
        for g in seqs:
            xp_scr[g, 0:8, :] = jnp.zeros((8, D_MPAD), F32)
            cb = cb_ref[g]
            m0 = m0_ref[g]
            n0 = n0_ref[g]
            for h in range(H_M):
                xp_scr[g, 8 - hist:8, h * LANES:h * LANES + HD_M] = cb[:, h * HD_M:(h + 1) * HD_M]
                c_scr[g, h] = jnp.zeros((LANES, LANES), F32)
                c_scr[g, h, :HD_M, :HD_M] = c0_ref[g, h]
                c_scr[g, h, HD_M:HD_M + 1, :HD_M] = n0[h:h + 1, :]
                m_scr[g, h:h + 1, :] = jnp.broadcast_to(m0[:, h:h + 1], (1, LANES))

    cw = cw_ref[...]
    pm_seq = [_seq_rows(pm_ref, g, L, lv, packed) for g in seqs]
    ucs = []
    for g in seqs:
        u_g = pm_seq[g][:, :D_MPAD]
        xp_scr[g, 8:8 + L, :] = u_g
        acc = cw[0:1] * xp_scr[g, 5:5 + L, :]
        acc = acc + cw[1:2] * xp_scr[g, 6:6 + L, :]
        acc = acc + cw[2:3] * xp_scr[g, 7:7 + L, :]
        ucs.append(acc + cw[3:4] * u_g)
    pm = _cat_rows(pm_seq)
    u = pm[:, :D_MPAD]
    og = pm[:, D_MPAD:]
    uc = _cat_rows(ucs) + cbias_ref[...]
    uc = uc * _sigmoid(uc)
    q = _mm(uc, wq_ref[...], NN, MODE_PROJ)
    k = _mm(uc, wk_ref[...], NN, MODE_PROJ)
    v = _mm(u, wv_ref[...], NN, MODE_PROJ)
    gp = (_mm(q, wgq_ref[...], NN, MODE_PROJ) + _mm(k, wgk_ref[...], NN, MODE_PROJ)
          + _mm(v, wgv_ref[...], NN, MODE_PROJ)) + bg_ref[...]
    k = k * (HD_M ** -0.5)

    ig = gp
    lf = -_softplus(-gp)
    if lv < L:
        valid = (lax.broadcasted_iota(jnp.int32, (GL, 1), 0) & (L - 1)) < lv
        ig = jnp.where(valid, ig, -jnp.inf)
        lf = jnp.where(valid, lf, 0.0)
    rg = lax.broadcasted_iota(jnp.int32, (GL, GL), 0)
    cg = lax.broadcasted_iota(jnp.int32, (GL, GL), 1)
    tri = jnp.logical_and(cg <= rg, (cg >> shift) == (rg >> shift))
    bcum = _masked_sum(tri, lf)
    lane = lax.broadcasted_iota(jnp.int32, (1, LANES), 1)
    gates = jnp.where(lane < H_M, ig, bcum)
    gates_t = gates.T
    head_lane = lane < HD_M
    one_lane = lane == HD_M
    ri = lax.broadcasted_iota(jnp.int32, (L, L), 0)
    ci = lax.broadcasted_iota(jnp.int32, (L, L), 1)
    causal = ci <= ri

    units = [(g, h) for g in seqs for h in range(H_M)]
    U = range(len(units))
    mm = functools.partial(_mm, mode=MODE_SCAN)
    rsl = [slice(g * L, (g + 1) * L) for g, _ in units]
    lsl = [slice(h * LANES, (h + 1) * LANES) for _, h in units]
    qh = [q[rsl[i], lsl[i]] for i in U]
    kh = [k[rsl[i], lsl[i]] for i in U]
    vh = [jnp.where(one_lane, 1.0, v[rsl[i], lsl[i]]) for i in U]
    i_c = [gates[rsl[i], h:h + 1] for i, (g, h) in enumerate(units)]
    b_c = [gates[rsl[i], H_M + h:H_M + h + 1] for i, (g, h) in enumerate(units)]
    i_r = [gates_t[h:h + 1, rsl[i]] for i, (g, h) in enumerate(units)]
    b_r = [gates_t[H_M + h:H_M + h + 1, rsl[i]] for i, (g, h) in enumerate(units)]
    m_prev = [m_scr[g, h:h + 1, 0:1] for g, h in units]
    c_h = [c_scr[g, h] for g, h in units]
    qc = [mm(qh[i], c_h[i], NT) for i in U]
    rowv = [i_r[i] - b_r[i] for i in U]
    inter = [b_c[i] + m_prev[i] for i in U]
    RB = min(L, MLSTM_ROW_BLOCK)
    hh = [[] for _ in U]
    m_last = [None for _ in U]
    for rb in range(L // RB):
        r0, r1 = rb * RB, (rb + 1) * RB
        cmask = causal[r0:r1, :r1]
        dm = [jnp.where(cmask, b_c[i][r0:r1] + rowv[i][:, :r1], -jnp.inf) for i in U]
        m_t = [jnp.maximum(inter[i][r0:r1], jnp.max(dm[i], axis=-1, keepdims=True)) for i in U]
        qk = [mm(qh[i][r0:r1], kh[i][:r1], NT) for i in U]
        s = [qk[i] * jnp.exp(dm[i] - m_t[i]) for i in U]
        w_i = [jnp.exp(inter[i][r0:r1] - m_t[i]) for i in U]
        num = [mm(s[i], vh[i][:r1], NN) + w_i[i] * qc[i][r0:r1] for i in U]
        for i in U:
            den = num[i][:, HD_M:HD_M + 1]
            hh[i].append(jnp.where(head_lane, num[i], 0.0) / jnp.maximum(jnp.abs(den), jnp.exp(-m_t[i])))
            m_last[i] = m_t[i][RB - 1:RB]
    hh = [_cat_rows(h) for h in hh]
    outs = []
    for i, (g, h) in enumerate(units):
        b_l = b_c[i][L - 1:L]
        m_new = m_last[i]
        w_s = jnp.exp(b_l - b_c[i] + i_c[i] - m_new)
        w_p = jnp.exp(b_l + m_prev[i] - m_new)
        ks = w_s * kh[i]
        c_scr[g, h] = w_p * c_h[i] + mm(vh[i], ks, TN)
        m_scr[g, h:h + 1, :] = jnp.broadcast_to(m_new, (1, LANES))
        mean = jnp.sum(hh[i], axis=-1, keepdims=True) * (1.0 / HD_M)
        hc = jnp.where(head_lane, hh[i] - mean, 0.0)
        var = jnp.sum(hc * hc, axis=-1, keepdims=True) * (1.0 / HD_M)
        outs.append(hc * lax.rsqrt(var + GN_EPS_M))

    hn = _cat_rows([jnp.concatenate(outs[g * H_M:(g + 1) * H_M], axis=1) for g in seqs])
    y = (hn * gnw_ref[...] + skip_ref[...] * uc) * _sigmoid(og)
    for g in seqs:
        _store_seq(y_ref, g, lv, packed, y[g * L:(g + 1) * L])

    @pl.when(c == pl.num_programs(1) - 1)
    def _():
        lane_h = lax.broadcasted_iota(jnp.int32, (1, H_M), 1)
        for g in seqs:
            rows = xp_scr[g, 8 + lv - hist:8 + lv, :]
            cbo_ref[g] = jnp.concatenate([rows[:, h * LANES:h * LANES + HD_M] for h in range(H_M)], axis=1)
            no_ref[g] = jnp.concatenate([c_scr[g, h, HD_M:HD_M + 1, :HD_M] for h in range(H_M)], axis=0)
            m_row = jnp.zeros((1, H_M), F32)
            for h in range(H_M):
                co_ref[g, h] = c_scr[g, h, :HD_M, :HD_M]
                m_row = jnp.where(lane_h == h, m_scr[g, h:h + 1, 0:1], m_row)
            mo_ref[g] = m_row

    for g in seqs:
        tail = xp_scr[g, 8 + L - hist:8 + L, :]
        xp_scr[g, 8 - hist:8, :] = tail


def _state_spec(G, shape, l):
    zeros = (0,) * len(shape)
    return pl.BlockSpec((None, G) + tuple(shape), lambda b, c: (l, b) + zeros)


def _param_spec(shape, l):
    zeros = (0,) * len(shape)
    return pl.BlockSpec((None,) + tuple(shape), lambda b, c: (l,) + zeros)


def _mixer_params():
    return pltpu.CompilerParams(dimension_semantics=("parallel", "arbitrary"), vmem_limit_bytes=VMEM_LIMIT)


class _SeqLayout:
    def __init__(self, B, T, chunk, step_rows, max_seqs):
        self.B, self.T = B, T
        self.packed = T < SHORT_SEQ_ROWS
        if self.packed:
            self.tb, self.L, self.nc = T, SHORT_SEQ_ROWS, 1
        else:
            self.tb = self.L = min(T, chunk)
            self.nc = T // self.L
            assert T % self.L == 0
        assert self.L & (self.L - 1) == 0
        self.G = max(1, min(B, step_rows // self.L, max_seqs))
        assert B % self.G == 0
        self.grid = (B // self.G, self.nc)
        self.static = dict(G=self.G, L=self.L, lv=self.tb, packed=self.packed)

    def view(self, a):
        shape = (self.B // self.G, self.G * self.T) if self.packed else (self.B, self.T)
        return a.reshape(shape + (a.shape[-1],))

    def spec(self, width):
        if self.packed:
            return pl.BlockSpec((1, self.G * self.T, width), lambda b, c: (b, 0, 0))
        return pl.BlockSpec((self.G, self.tb, width), lambda b, c: (b, c, 0))

    def out_shape(self, width):
        shape = (self.B // self.G, self.G * self.T) if self.packed else (self.B, self.T)
        return jax.ShapeDtypeStruct(shape + (width,), F32)


def _rwkv_mixer(pr, st, W, l, B, T):
    lay = _SeqLayout(B, T, RWKV_CHUNK, RWKV_STEP_ROWS, RWKV_MAX_SEQS)
    G, L = lay.G, lay.L
    vec = lambda n: _param_spec((1, n), l)
    shift = _state_spec(G, (1, P_RWKV), l)
    wkv = _state_spec(G, (H_R, HD_R, HD_R), l)
    y, st_shift, st_wkv = pl.pallas_call(
        functools.partial(_rwkv_body, **lay.static),
        grid=lay.grid,
        in_specs=[lay.spec(P_RWKV), shift, wkv,
                  vec(P_RWKV), vec(D_RWKV), _param_spec((LANES, D_RWKV), l), vec(D_RWKV),
                  _param_spec((LANES, D_RWKV), l), _param_spec((LANES, D_RWKV), l),
                  vec(D_RWKV), vec(D_RWKV), vec(D_RWKV), vec(D_RWKV), vec(D_RWKV)],
        out_specs=[lay.spec(D_RWKV), shift, wkv],
        out_shape=[lay.out_shape(D_RWKV), jax.ShapeDtypeStruct(st["shift"].shape, F32),
                   jax.ShapeDtypeStruct(st["wkv"].shape, F32)],
        input_output_aliases={1: 1, 2: 2},
        scratch_shapes=[pltpu.VMEM((G, 8 + L, P_RWKV), F32), pltpu.VMEM((G, N_PAIR, LANES, LANES), F32)],
        compiler_params=_mixer_params(),
        name="rwkv7",
    )(lay.view(pr), st["shift"], st["wkv"], W["rwkv_mu"], W["rwkv_w0"], W["rwkv_w_up"], W["rwkv_a0"],
      W["rwkv_a_up"], W["rwkv_g_up"], W["rwkv_k_k"], W["rwkv_k_a"], W["rwkv_r_k"], W["rwkv_ln_w"], W["rwkv_ln_b"])
    return y.reshape(B * T, D_RWKV), st_shift, st_wkv


def _mlstm_conv_mixer(pm, pc, st, W, l, B, T):
    lay = _SeqLayout(B, T, MLSTM_CHUNK, MLSTM_STEP_ROWS, MLSTM_MAX_SEQS)
    G, L = lay.G, lay.L
    hist = MCONV_W - 1
    vec = lambda n: _param_spec((1, n), l)
    sq = _param_spec((D_MPAD, D_MPAD), l)
    gate = _param_spec((D_MPAD, LANES), l)
    states = [_state_spec(G, (hist, D_MLSTM), l), _state_spec(G, (H_M, HD_M, HD_M), l),
              _state_spec(G, (H_M, HD_M), l), _state_spec(G, (1, H_M), l),
              _state_spec(G, (CONV_W - 1, D_CONV), l)]
    names = ("mconv", "mC", "mn", "mm", "conv")
    outs = pl.pallas_call(
        functools.partial(_mlstm_body, **lay.static),
        grid=lay.grid,
        in_specs=[lay.spec(2 * D_MPAD), lay.spec(3 * D_CONV)] + states
                 + [_param_spec((MCONV_W, D_MPAD), l), vec(D_MPAD), sq, sq, sq, gate, gate, gate, vec(LANES),
                    vec(D_MPAD), vec(D_MPAD), _param_spec((CONV_W, D_CONV), l)],
        out_specs=[lay.spec(D_MPAD), lay.spec(D_CONV)] + states,
        out_shape=[lay.out_shape(D_MPAD), lay.out_shape(D_CONV)]
                  + [jax.ShapeDtypeStruct(st[n].shape, F32) for n in names],
        input_output_aliases={2: 2, 3: 3, 4: 4, 5: 5, 6: 6},
        scratch_shapes=[pltpu.VMEM((G, 8 + L, D_MPAD), F32), pltpu.VMEM((G, H_M, LANES, LANES), F32),
                        pltpu.VMEM((G, H_M, LANES), F32), pltpu.VMEM((G, 8 + L, D_CONV), F32)],
        compiler_params=_mixer_params(),
        name="mlstm_conv",
    )(lay.view(pm), lay.view(pc), *[st[n] for n in names], W["mlstm_conv_w"], W["mlstm_conv_b"], W["mlstm_wq"],
      W["mlstm_wk"], W["mlstm_wv"], W["mlstm_wgq"], W["mlstm_wgk"], W["mlstm_wgv"], W["mlstm_b_gate"],
      W["mlstm_gn_w"], W["mlstm_skip"], W["conv_w"])
    return (outs[0].reshape(B * T, D_MPAD), outs[1].reshape(B * T, D_CONV)) + tuple(outs[2:])


def _pad_heads(x, axis):
    axis = axis % x.ndim
    shp = x.shape
    x = x.reshape(shp[:axis] + (H_M, HD_M) + shp[axis + 1:])
    pad = [(0, 0)] * x.ndim
    pad[axis + 1] = (0, LANES - HD_M)
    x = jnp.pad(x, pad)
    return x.reshape(shp[:axis] + (D_MPAD,) + shp[axis + 1:])


def _head_block_diag(w):
    wp = jnp.pad(w, ((0, 0), (0, 0), (0, LANES - HD_M), (0, LANES - HD_M)))
    eye = jnp.eye(H_M, dtype=w.dtype)
    return jnp.einsum("lhde,hg->lhdge", wp, eye).reshape(w.shape[0], D_MPAD, D_MPAD)


def _prep_weights(norm_ff, ffn_gate, ffn_up, ffn_down, norm_mix, w_in, w_out, conv_w,
                  rwkv_mu, rwkv_w0, rwkv_w_up, rwkv_a0, rwkv_a_up, rwkv_g_up, rwkv_k_k, rwkv_k_a, rwkv_r_k,
                  rwkv_ln_w, rwkv_ln_b, mlstm_conv_w, mlstm_conv_b, mlstm_wq, mlstm_wk, mlstm_wv,
                  mlstm_w_gate, mlstm_b_gate, mlstm_gn_w, mlstm_skip, norm_final):
    depth = w_in.shape[0]
    pc, pr = 3 * D_CONV, P_RWKV
    row = lambda x: x[:, None, :]
    W = {}
    W["norm_ff"] = norm_ff[:, :, None, :]
    W["ffn_gate"] = ffn_gate.astype(BF16)
    W["ffn_up"] = ffn_up.astype(BF16)
    W["ffn_down"] = ffn_down.astype(BF16)
    W["norm_mix"] = row(norm_mix)
    W["w_in_c"] = w_in[:, :, :pc].astype(BF16)
    W["w_in_r"] = w_in[:, :, pc:pc + pr].astype(BF16)
    w_u = _pad_heads(w_in[:, :, pc + pr:pc + pr + D_MLSTM], 2)
    w_o = _pad_heads(w_in[:, :, pc + pr + D_MLSTM:], 2)
    W["w_in_m"] = jnp.concatenate([w_u, w_o], axis=2).astype(BF16)
    W["w_out_c"] = w_out[:, :D_CONV].astype(BF16)
    W["w_out_r"] = w_out[:, D_CONV:D_CONV + D_RWKV].astype(BF16)
    W["w_out_m"] = _pad_heads(w_out[:, D_CONV + D_RWKV:], 1).astype(BF16)
    W["conv_w"] = conv_w
    W["rwkv_mu"] = row(rwkv_mu)
    W["rwkv_w0"] = row(rwkv_w0)
    W["rwkv_a0"] = row(rwkv_a0)
    zl = lambda n: jnp.zeros((depth, n, D_RWKV), F32)
    W["rwkv_w_up"] = jnp.concatenate([rwkv_w_up, zl(LANES - LORA_W)], axis=1)
    W["rwkv_a_up"] = jnp.concatenate([zl(LORA_W), rwkv_a_up, zl(LORA_G)], axis=1)
    W["rwkv_g_up"] = jnp.concatenate([zl(LORA_W + LORA_A), rwkv_g_up], axis=1)
    W["rwkv_k_k"] = row(rwkv_k_k)
    W["rwkv_k_a"] = row(rwkv_k_a)
    W["rwkv_r_k"] = row(rwkv_r_k.reshape(depth, D_RWKV))
    W["rwkv_ln_w"] = row(rwkv_ln_w)
    W["rwkv_ln_b"] = row(rwkv_ln_b)
    W["mlstm_conv_w"] = _pad_heads(mlstm_conv_w, 2)
    W["mlstm_conv_b"] = row(_pad_heads(mlstm_conv_b, 1))
    W["mlstm_wq"] = _head_block_diag(mlstm_wq)
    W["mlstm_wk"] = _head_block_diag(mlstm_wk)
    W["mlstm_wv"] = _head_block_diag(mlstm_wv)
    wg = mlstm_w_gate.reshape(depth, H_M, 3, HD_M, 2 * H_M)
    for i, name in enumerate(("mlstm_wgq", "mlstm_wgk", "mlstm_wgv")):
        part = jnp.pad(wg[:, :, i], ((0, 0), (0, 0), (0, LANES - HD_M), (0, LANES - 2 * H_M)))
        W[name] = part.reshape(depth, D_MPAD, LANES)
    W["mlstm_b_gate"] = row(jnp.pad(mlstm_b_gate, ((0, 0), (0, LANES - 2 * H_M))))
    W["mlstm_gn_w"] = row(_pad_heads(mlstm_gn_w, 1))
    W["mlstm_skip"] = row(_pad_heads(mlstm_skip, 1))
    W["norm_final"] = norm_final[None, :]
    return W


_STATE_NAMES = ("conv", "shift", "wkv", "mconv", "mC", "mn", "mm")


def _layer_mixers(p, st, W, l, B, T):
    pc, pr, pm = p
    yr, st["shift"], st["wkv"] = _rwkv_mixer(pr, st, W, l, B, T)
    ym, yc, st["mconv"], st["mC"], st["mn"], st["mm"], st["conv"] = _mlstm_conv_mixer(pm, pc, st, W, l, B, T)
    return yc, yr, ym


def _run_trunks(x_p, states_p, x_s, states_s, W):
    (Bp, Tp, D), (Bs, Ts, _) = x_p.shape, x_s.shape
    depth = W["w_in_c"].shape[0]
    st_p, st_s = dict(zip(_STATE_NAMES, states_p)), dict(zip(_STATE_NAMES, states_s))
    for st in (st_p, st_s):
        st["mm"] = st["mm"][:, :, None, :]
    xp, xs = x_p.reshape(Bp * Tp, D), x_s.reshape(Bs * Ts, D)
    for l in range(depth):
        xp, xs = _ffn(xp, xs, W, l, 0)
        p_p, p_s = _inproj(xp, xs, W, l)
        y_p = _layer_mixers(p_p, st_p, W, l, Bp, Tp)
        y_s = _layer_mixers(p_s, st_s, W, l, Bs, Ts)
        xp, xs = _mix_ffn(xp, y_p, xs, y_s, W, l, final=(l == depth - 1))
    for st in (st_p, st_s):
        st["mm"] = st["mm"][:, :, 0, :]
    return ((xp.reshape(Bp, Tp, D), xs.reshape(Bs, Ts, D))
            + tuple(st_p[n] for n in _STATE_NAMES) + tuple(st_s[n] for n in _STATE_NAMES))


def kernel(x_prompt, x_sample, cache_conv, cache_shift, state_wkv, cache_mconv, state_mC, state_mn, state_mm, norm_ff, ffn_gate, ffn_up, ffn_down, norm_mix, w_in, w_out, conv_w, rwkv_mu, rwkv_w0, rwkv_w_up, rwkv_a0, rwkv_a_up, rwkv_g_up, rwkv_k_k, rwkv_k_a, rwkv_r_k, rwkv_ln_w, rwkv_ln_b, mlstm_conv_w, mlstm_conv_b, mlstm_wq, mlstm_wk, mlstm_wv, mlstm_w_gate, mlstm_b_gate, mlstm_gn_w, mlstm_skip, norm_final):
    W = _prep_weights(norm_ff, ffn_gate, ffn_up, ffn_down, norm_mix, w_in, w_out, conv_w,
                      rwkv_mu, rwkv_w0, rwkv_w_up, rwkv_a0, rwkv_a_up, rwkv_g_up, rwkv_k_k, rwkv_k_a, rwkv_r_k,
                      rwkv_ln_w, rwkv_ln_b, mlstm_conv_w, mlstm_conv_b, mlstm_wq, mlstm_wk, mlstm_wv,
                      mlstm_w_gate, mlstm_b_gate, mlstm_gn_w, mlstm_skip, norm_final)
    depth = w_in.shape[0]
    Bp = x_prompt.shape[0]
    zeros = lambda *shape: jnp.zeros((depth, Bp) + shape, F32)
    st_p = (zeros(CONV_W - 1, D_CONV), zeros(1, P_RWKV), zeros(H_R, HD_R, HD_R), zeros(MCONV_W - 1, D_MLSTM),
            zeros(H_M, HD_M, HD_M), zeros(H_M, HD_M), zeros(H_M))
    st_s = (cache_conv, cache_shift, state_wkv, cache_mconv, state_mC, state_mn, state_mm)
    return _run_trunks(x_prompt, st_p, x_sample, st_s, W)
```

```python
import functools
import math

import jax
import jax.numpy as jnp
from jax import lax
from jax.experimental import pallas as pl
from jax.experimental.pallas import tpu as pltpu

F32 = jnp.float32
BF16 = jnp.bfloat16

LANES = 128
H_R, HD_R = 6, 64
D_RWKV = H_R * HD_R
N_PAIR = H_R // 2
LORA_W, LORA_A, LORA_G = 32, 32, 64
P_RWKV = 3 * D_RWKV + LORA_W + LORA_A + LORA_G
H_M, HD_M = 4, 96
D_MLSTM = H_M * HD_M
D_MPAD = H_M * LANES
D_CONV = 256
CONV_W, MCONV_W = 3, 4
RMS_EPS = 1e-6
GN_EPS_R = 64e-5
GN_EPS_M = 1e-6
SHORT_SEQ_ROWS = 8
RWKV_CHUNK = 64
RWKV_STEP_ROWS = 256
MLSTM_CHUNK = 256
MLSTM_STEP_ROWS = 256
MLSTM_ROW_BLOCK = 128
RWKV_MAX_SEQS = 16
MLSTM_MAX_SEQS = 8
FUSED_MAX_SEQS = 8
TOKEN_TILE = 512
FFN_COLS = 256
VMEM_LIMIT = 52 * 1024 * 1024

NN = ((1,), (0,))
NT = ((1,), (1,))
TN = ((0,), (0,))

MODE_PROJ = "b1"
MODE_SCAN = "b1"


def _mm(a, b, dims=NN, mode="b1"):
    dn = (dims, ((), ()))
    dot = lambda x, y: lax.dot_general(x, y, dn, preferred_element_type=F32)
    ah = a.astype(BF16)
    bh = b.astype(BF16)
    if mode == "b1":
        return dot(ah, bh)
    al = (a - ah.astype(F32)).astype(BF16)
    bl = (b - bh.astype(F32)).astype(BF16)
    return dot(ah, bh) + (dot(ah, bl) + dot(al, bh))


def _rms(x, g):
    return x * lax.rsqrt(jnp.mean(x * x, axis=-1, keepdims=True) + RMS_EPS) * g


def _sigmoid(x):
    return 1.0 / (1.0 + jnp.exp(-x))


def _softplus(x):
    return jnp.maximum(x, 0.0) + jnp.log1p(jnp.exp(-jnp.abs(x)))


def _masked_sum(mask, x):
    m = mask.astype(BF16)
    x1 = x.astype(BF16)
    r1 = x - x1.astype(F32)
    x2 = r1.astype(BF16)
    x3 = (r1 - x2.astype(F32)).astype(BF16)
    dot = lambda y: jnp.dot(m, y, preferred_element_type=F32)
    return dot(x1) + (dot(x2) + dot(x3))


def _seq_rows(ref, g, L, tb, packed):
    x = ref[0, g * tb:(g + 1) * tb, :] if packed else ref[g]
    if tb < L:
        x = jnp.concatenate([x, jnp.zeros((L - tb, x.shape[1]), x.dtype)], axis=0)
    return x


def _store_seq(ref, g, tb, packed, rows):
    if packed:
        ref[0, g * tb:(g + 1) * tb, :] = rows[:tb]
    else:
        ref[g] = rows[:tb]


def _cat_rows(xs):
    return xs[0] if len(xs) == 1 else jnp.concatenate(xs, axis=0)


def _half_ffn(x, g_ref, wg_ref, wu_ref, wd_ref):
    n = _rms(x, g_ref[...]).astype(BF16)
    acc = jnp.zeros(x.shape, F32)
    for c in range(wg_ref.shape[1] // FFN_COLS):
        sl = slice(c * FFN_COLS, (c + 1) * FFN_COLS)
        hg = jnp.dot(n, wg_ref[:, sl], preferred_element_type=F32)
        hu = jnp.dot(n, wu_ref[:, sl], preferred_element_type=F32)
        h = (hg * _sigmoid(hg) * hu).astype(BF16)
        acc = acc + jnp.dot(h, wd_ref[sl, :], preferred_element_type=F32)
    return x + 0.5 * acc


def _ffn_body(x_ref, g_ref, wg_ref, wu_ref, wd_ref, o_ref):
    o_ref[...] = _half_ffn(x_ref[...], g_ref, wg_ref, wu_ref, wd_ref)


def _mix_ffn_body(x_ref, yc_ref, yr_ref, ym_ref, wc_ref, wr_ref, wm_ref, g_ref, wg_ref, wu_ref, wd_ref,
                  gf_ref, o_ref, *, final):
    acc = jnp.dot(yc_ref[...].astype(BF16), wc_ref[...], preferred_element_type=F32)
    acc = acc + jnp.dot(yr_ref[...].astype(BF16), wr_ref[...], preferred_element_type=F32)
    acc = acc + jnp.dot(ym_ref[...].astype(BF16), wm_ref[...], preferred_element_type=F32)
    x = _half_ffn(x_ref[...] + acc, g_ref, wg_ref, wu_ref, wd_ref)
    o_ref[...] = _rms(x, gf_ref[...]) if final else x


def _inproj_body(x_ref, g_ref, wc_ref, wr_ref, wm_ref, pc_ref, pr_ref, pm_ref):
    n = _rms(x_ref[...], g_ref[...]).astype(BF16)
    pc_ref[...] = jnp.dot(n, wc_ref[...], preferred_element_type=F32)
    pr_ref[...] = jnp.dot(n, wr_ref[...], preferred_element_type=F32)
    pm_ref[...] = jnp.dot(n, wm_ref[...], preferred_element_type=F32)


def _resident(shape, *lead):
    nlead = len(lead)
    zeros = (0,) * len(shape)
    return pl.BlockSpec((None,) * nlead + tuple(shape), lambda i: tuple(lead) + zeros,
                        pipeline_mode=pl.Buffered(1))


def _two_trunk_body(*refs, inner, n_x, n_par, n_out, tiles):
    xs_p, xs_s = refs[:n_x], refs[n_x:2 * n_x]
    params = refs[2 * n_x:2 * n_x + n_par]
    outs = refs[2 * n_x + n_par:]
    i = pl.program_id(0)

    @pl.when(i < tiles)
    def _():
        inner(*xs_p, *params, *outs[:n_out])

    @pl.when(i == tiles)
    def _():
        inner(*xs_s, *params, *outs[n_out:])


def _token_call(inner, xs_p, xs_s, params, param_specs, out_widths, name):
    n_p, n_s = xs_p[0].shape[0], xs_s[0].shape[0]
    tm = min(TOKEN_TILE, n_p)
    tiles = n_p // tm
    assert n_p % tm == 0
    rows_p = lambda w: pl.BlockSpec((tm, w), lambda i: (jnp.minimum(i, tiles - 1), 0))
    rows_s_in = lambda w: pl.BlockSpec((n_s, w), lambda i: (0, 0), pipeline_mode=pl.Buffered(1))
    rows_s_out = lambda w: pl.BlockSpec((n_s, w), lambda i: (0, 0))
    outs = pl.pallas_call(
        functools.partial(_two_trunk_body, inner=inner, n_x=len(xs_p), n_par=len(params),
                          n_out=len(out_widths), tiles=tiles),
        grid=(tiles + 1,),
        in_specs=[rows_p(a.shape[1]) for a in xs_p] + [rows_s_in(a.shape[1]) for a in xs_s] + list(param_specs),
        out_specs=[rows_p(w) for w in out_widths] + [rows_s_out(w) for w in out_widths],
        out_shape=[jax.ShapeDtypeStruct((n_p, w), F32) for w in out_widths]
                  + [jax.ShapeDtypeStruct((n_s, w), F32) for w in out_widths],
        compiler_params=pltpu.CompilerParams(dimension_semantics=("arbitrary",), vmem_limit_bytes=VMEM_LIMIT),
        name=name,
    )(*xs_p, *xs_s, *params)
    k = len(out_widths)
    return outs[:k], outs[k:]


def _ffn(xp, xs, W, l, which):
    d = xp.shape[1]
    f = W["ffn_gate"].shape[-1]
    specs = [_resident((1, d), l, which), _resident((d, f), l, which), _resident((d, f), l, which),
             _resident((f, d), l, which)]
    (op,), (os_,) = _token_call(_ffn_body, [xp], [xs], [W["norm_ff"], W["ffn_gate"], W["ffn_up"], W["ffn_down"]],
                                specs, [d], "ffn")
    return op, os_


def _inproj(xp, xs, W, l):
    d = xp.shape[1]
    ws = [W["w_in_c"], W["w_in_r"], W["w_in_m"]]
    widths = [w.shape[-1] for w in ws]
    specs = [_resident((1, d), l)] + [_resident((d, w), l) for w in widths]
    return _token_call(_inproj_body, [xp], [xs], [W["norm_mix"]] + ws, specs, widths, "inproj")


def _mix_ffn(xp, yp, xs, ys, W, l, final):
    d = xp.shape[1]
    f = W["ffn_gate"].shape[-1]
    ws = [W["w_out_c"], W["w_out_r"], W["w_out_m"]]
    specs = ([_resident(w.shape[1:], l) for w in ws]
             + [_resident((1, d), l, 1), _resident((d, f), l, 1), _resident((d, f), l, 1),
                _resident((f, d), l, 1), _resident((1, d))])
    params = ws + [W["norm_ff"], W["ffn_gate"], W["ffn_up"], W["ffn_down"], W["norm_final"]]
    (op,), (os_,) = _token_call(functools.partial(_mix_ffn_body, final=final), [xp] + list(yp), [xs] + list(ys),
                                params, specs, [d], "mix_ffn")
    return op, os_


def _short_conv(pc_ref, buf_ref, w_ref, y_ref, nb_ref, xp_scr, c, is_last, *, G, L, lv, packed):
    hist = CONV_W - 1
    w = w_ref[...]
    for g in range(G):
        @pl.when(c == 0)
        def _():
            xp_scr[g, 8 - hist:8, :] = buf_ref[g]

        p = _seq_rows(pc_ref, g, L, lv, packed)
        b_gate = p[:, :D_CONV]
        xp_scr[g, 8:8 + L, :] = p[:, D_CONV:2 * D_CONV] * p[:, 2 * D_CONV:]
        z = w[0:1] * xp_scr[g, 6:6 + L, :]
        z = z + w[1:2] * xp_scr[g, 7:7 + L, :]
        z = z + w[2:3] * xp_scr[g, 8:8 + L, :]
        _store_seq(y_ref, g, lv, packed, b_gate * z)

        @pl.when(is_last)
        def _():
            nb_ref[g] = xp_scr[g, 8 + lv - hist:8 + lv, :]

        tail = xp_scr[g, 8 + L - hist:8 + L, :]
        xp_scr[g, 8 - hist:8, :] = tail


def _rwkv_steps(pr_ref, sh_ref, s0_ref, mu_ref, w0_ref, wup_ref, a0_ref, aup_ref, gup_ref, kk_ref, ka_ref,
                rk_ref, lnw_ref, lnb_ref, y_ref, sho_ref, so_ref, xp_scr, s_scr, c, is_last, *, G, L, lv, packed):
    GL = G * L
    shift = L.bit_length() - 1
    seqs = range(G)

    @pl.when(c == 0)
    def _():
        for g in seqs:
            xp_scr[g, 7:8, :] = sh_ref[g]
            for j in range(N_PAIR):
                s_scr[g, j] = jnp.zeros((LANES, LANES), F32)
                s_scr[g, j, :HD_R, :HD_R] = s0_ref[g, 2 * j]
                s_scr[g, j, HD_R:, HD_R:] = s0_ref[g, 2 * j + 1]

    p_seq = [_seq_rows(pr_ref, g, L, lv, packed) for g in seqs]
    for g in seqs:
        xp_scr[g, 8:8 + L, :] = p_seq[g]
    p = _cat_rows(p_seq)
    prev = _cat_rows([xp_scr[g, 7:7 + L, :] for g in seqs])
    ps = p + (prev - p) * mu_ref[...]
    r = ps[:, 0:D_RWKV]
    k = ps[:, D_RWKV:2 * D_RWKV]
    v = ps[:, 2 * D_RWKV:3 * D_RWKV]
    x4 = ps[:, 3 * D_RWKV:]
    lw = -math.exp(-0.5) * _sigmoid(w0_ref[...] + _mm(jnp.tanh(x4), wup_ref[...], NN, MODE_PROJ))
    a = _sigmoid(a0_ref[...] + _mm(x4, aup_ref[...], NN, MODE_PROJ))
    gate = _mm(_sigmoid(x4), gup_ref[...], NN, MODE_PROJ)
    yield

    lane = lax.broadcasted_iota(jnp.int32, (1, LANES), 1)
    lo = lane < HD_R

    def pair_sum(x):
        s0 = jnp.sum(jnp.where(lo, x, 0.0), axis=-1, keepdims=True)
        s1 = jnp.sum(jnp.where(lo, 0.0, x), axis=-1, keepdims=True)
        return jnp.where(lo, s0, s1)

    def head_sum(x):
        return jnp.concatenate([pair_sum(x[:, j * LANES:(j + 1) * LANES]) for j in range(N_PAIR)], axis=1)

    kk = k * kk_ref[...]
    kk = kk * lax.rsqrt(jnp.maximum(head_sum(kk * kk), 1e-24))
    k2 = k * (1.0 + (a - 1.0) * ka_ref[...])
    bonus = head_sum(r * k2 * rk_ref[...]) * v

    if lv < L:
        valid = (lax.broadcasted_iota(jnp.int32, (GL, 1), 0) & (L - 1)) < lv
        lw = jnp.where(valid, lw, 0.0)
        kk = jnp.where(valid, kk, 0.0)
        k2 = jnp.where(valid, k2, 0.0)

    ri = lax.broadcasted_iota(jnp.int32, (GL, GL), 0)
    ci = lax.broadcasted_iota(jnp.int32, (GL, GL), 1)
    tri = jnp.logical_and(ci <= ri, (ci >> shift) == (ri >> shift))
    cl = _masked_sum(tri, lw)
    e_pos = jnp.exp(cl)
    e_neg = jnp.exp(-cl)
    a_t = -kk * jnp.exp(cl - lw)
    b_t = kk * a * e_neg
    k_t = k2 * e_neg
    r_t = r * e_pos
    yield

    L2 = 2 * L
    ri2 = lax.broadcasted_iota(jnp.int32, (L2, L2), 0)
    ci2 = lax.broadcasted_iota(jnp.int32, (L2, L2), 1)
    tri2 = ri2 & (L - 1)
    tci2 = ci2 & (L - 1)
    strict = tci2 < tri2
    incl = tci2 <= tri2
    eye = (ri2 == ci2).astype(F32)
    row_lo = lax.broadcasted_iota(jnp.int32, (L2, 1), 0) < L
    own = jnp.logical_and(row_lo, lo) | jnp.logical_and(jnp.logical_not(row_lo), jnp.logical_not(lo))

    def stack(x):
        return jnp.concatenate([jnp.where(lo, x, 0.0), jnp.where(lo, 0.0, x)], axis=0)

    units = [(g, j) for g in seqs for j in range(N_PAIR)]
    U = range(len(units))
    rsl = [slice(g * L, (g + 1) * L) for g, _ in units]
    lsl = [slice(j * LANES, (j + 1) * LANES) for _, j in units]
    mm = functools.partial(_mm, mode=MODE_SCAN)
    xa, xb, xk, xr, xv = ([stack(t[rsl[i], lsl[i]]) for i in U] for t in (a_t, b_t, k_t, r_t, v))
    s_bd = [s_scr[g, j] for g, j in units]
    xar = [jnp.concatenate([xa[i], xr[i]], axis=0) for i in U]
    gram = [mm(xar[i], jnp.concatenate([xb[i], xk[i]], axis=0), NT) for i in U]
    a_ab = [jnp.where(strict, gm[:L2, :L2], 0.0) for gm in gram]
    a_ak = [jnp.where(strict, gm[:L2, L2:], 0.0) for gm in gram]
    m_rb = [jnp.where(incl, gm[L2:, :L2], 0.0) for gm in gram]
    m_rk = [jnp.where(incl, gm[L2:, L2:], 0.0) for gm in gram]
    yield
    sh = [mm(xar[i], s_bd[i], NT) for i in U]
    akv = [sh[i] + mm(jnp.concatenate([a_ak[i], m_rk[i]], axis=0), xv[i], NN) for i in U]
    yield
    tinv = [eye + m for m in a_ab]
    steps = L.bit_length() - 2
    if steps > 0:
        pw = [mm(m, m, NN) for m in a_ab]
    for step in range(steps):
        if step < steps - 1:
            both = [mm(pw[i], jnp.concatenate([tinv[i], pw[i]], axis=1), NN) for i in U]
            tinv = [tinv[i] + both[i][:, :L2] for i in U]
            pw = [both[i][:, L2:] for i in U]
        else:
            tinv = [tinv[i] + mm(pw[i], tinv[i], NN) for i in U]
        yield
    u = [mm(tinv[i], akv[i][:L2], NN) for i in U]
    yield
    o = [akv[i][L2:] + mm(m_rb[i], u[i], NN) for i in U]
    yield
    for i, (g, j) in enumerate(units):
        g_last = e_pos[g * L + L - 1:g * L + L, lsl[i]]
        upd = mm(jnp.concatenate([u[i], xv[i]], axis=0), jnp.concatenate([xb[i], xk[i]], axis=0), TN)
        s_scr[g, j] = (s_bd[i] + upd) * g_last
    yield
    outs = []
    for i in U:
        mean = jnp.sum(o[i], axis=-1, keepdims=True) * (1.0 / HD_R)
        oc = jnp.where(own, o[i] - mean, 0.0)
        var = jnp.sum(oc * oc, axis=-1, keepdims=True) * (1.0 / HD_R)
        on = oc * lax.rsqrt(var + GN_EPS_R)
        outs.append(on[:L] + on[L:])

    yn = _cat_rows([jnp.concatenate(outs[g * N_PAIR:(g + 1) * N_PAIR], axis=1) for g in seqs])
    y = (yn * lnw_ref[...] + lnb_ref[...] + bonus) * gate
    for g in seqs:
        _store_seq(y_ref, g, lv, packed, y[g * L:(g + 1) * L])

    @pl.when(is_last)
    def _():
        for g in seqs:
            sho_ref[g] = xp_scr[g, 8 + lv - 1:8 + lv, :]
            for j in range(N_PAIR):
                so_ref[g, 2 * j] = s_scr[g, j, :HD_R, :HD_R]
                so_ref[g, 2 * j + 1] = s_scr[g, j, HD_R:, HD_R:]

    for g in seqs:
        xp_scr[g, 7:8, :] = xp_scr[g, 7 + L:8 + L, :]


def _mlstm_steps(pm_ref, pc_ref, cb_ref, c0_ref, n0_ref, m0_ref, sc_ref, cw_ref, cbias_ref, wq_ref, wk_ref, wv_ref,
                 wgq_ref, wgk_ref, wgv_ref, bg_ref, gnw_ref, skip_ref, scw_ref,
                 y_ref, yc_ref, cbo_ref, co_ref, no_ref, mo_ref, sco_ref, xp_scr, c_scr, m_scr, xpc_scr,
                 c, is_last, *, G, L, lv, packed):
    _short_conv(pc_ref, sc_ref, scw_ref, yc_ref, sco_ref, xpc_scr, c, is_last, G=G, L=L, lv=lv, packed=packed)
    yield
    hist = MCONV_W - 1
    GL = G * L
    shift = L.bit_length() - 1
    seqs = range(G)

    @pl.when(c == 0)
    def _():
        for g in seqs:
            xp_scr[g, 0:8, :] = jnp.zeros((8, D_MPAD), F32)
            cb = cb_ref[g]
            m0 = m0_ref[g]
            n0 = n0_ref[g]
            for h in range(H_M):
                xp_scr[g, 8 - hist:8, h * LANES:h * LANES + HD_M] = cb[:, h * HD_M:(h + 1) * HD_M]
                c_scr[g, h] = jnp.zeros((LANES, LANES), F32)
                c_scr[g, h, :HD_M, :HD_M] = c0_ref[g, h]
                c_scr[g, h, HD_M:HD_M + 1, :HD_M] = n0[h:h + 1, :]
                m_scr[g, h:h + 1, :] = jnp.broadcast_to(m0[:, h:h + 1], (1, LANES))

    cw = cw_ref[...]
    pm_seq = [_seq_rows(pm_ref, g, L, lv, packed) for g in seqs]
    ucs = []
    for g in seqs:
        u_g = pm_seq[g][:, :D_MPAD]
        xp_scr[g, 8:8 + L, :] = u_g
        acc = cw[0:1] * xp_scr[g, 5:5 + L, :]
        acc = acc + cw[1:2] * xp_scr[g, 6:6 + L, :]
        acc = acc + cw[2:3] * xp_scr[g, 7:7 + L, :]
        ucs.append(acc + cw[3:4] * u_g)
    pm = _cat_rows(pm_seq)
    u = pm[:, :D_MPAD]
    og = pm[:, D_MPAD:]
    uc = _cat_rows(ucs) + cbias_ref[...]
    uc = uc * _sigmoid(uc)
    yield
    q = _mm(uc, wq_ref[...], NN, MODE_PROJ)
    k = _mm(uc, wk_ref[...], NN, MODE_PROJ)
    v = _mm(u, wv_ref[...], NN, MODE_PROJ)
    gp = (_mm(q, wgq_ref[...], NN, MODE_PROJ) + _mm(k, wgk_ref[...], NN, MODE_PROJ)
          + _mm(v, wgv_ref[...], NN, MODE_PROJ)) + bg_ref[...]
    k = k * (HD_M ** -0.5)
    yield

    ig = gp
    lf = -_softplus(-gp)
    if lv < L:
        valid = (lax.broadcasted_iota(jnp.int32, (GL, 1), 0) & (L - 1)) < lv
        ig = jnp.where(valid, ig, -jnp.inf)
        lf = jnp.where(valid, lf, 0.0)
    rg = lax.broadcasted_iota(jnp.int32, (GL, GL), 0)
    cg = lax.broadcasted_iota(jnp.int32, (GL, GL), 1)
    tri = jnp.logical_and(cg <= rg, (cg >> shift) == (rg >> shift))
    bcum = _masked_sum(tri, lf)
    lane = lax.broadcasted_iota(jnp.int32, (1, LANES), 1)
    gates = jnp.where(lane < H_M, ig, bcum)
    gates_t = gates.T
    yield
    head_lane = lane < HD_M
    one_lane = lane == HD_M
    ri = lax.broadcasted_iota(jnp.int32, (L, L), 0)
    ci = lax.broadcasted_iota(jnp.int32, (L, L), 1)
    causal = ci <= ri

    units = [(g, h) for g in seqs for h in range(H_M)]
    U = range(len(units))
    mm = functools.partial(_mm, mode=MODE_SCAN)
    rsl = [slice(g * L, (g + 1) * L) for g, _ in units]
    lsl = [slice(h * LANES, (h + 1) * LANES) for _, h in units]
    qh = [q[rsl[i], lsl[i]] for i in U]
    kh = [k[rsl[i], lsl[i]] for i in U]
    vh = [jnp.where(one_lane, 1.0, v[rsl[i], lsl[i]]) for i in U]
    i_c = [gates[rsl[i], h:h + 1] for i, (g, h) in enumerate(units)]
    b_c = [gates[rsl[i], H_M + h:H_M + h + 1] for i, (g, h) in enumerate(units)]
    i_r = [gates_t[h:h + 1, rsl[i]] for i, (g, h) in enumerate(units)]
    b_r = [gates_t[H_M + h:H_M + h + 1, rsl[i]] for i, (g, h) in enumerate(units)]
    m_prev = [m_scr[g, h:h + 1, 0:1] for g, h in units]
    c_h = [c_scr[g, h] for g, h in units]
    qc = [mm(qh[i], c_h[i], NT) for i in U]
    rowv = [i_r[i] - b_r[i] for i in U]
    inter = [b_c[i] + m_prev[i] for i in U]
    yield
    RB = min(L, MLSTM_ROW_BLOCK)
    hh = [[] for _ in U]
    m_last = [None for _ in U]
    for rb in range(L // RB):
        r0, r1 = rb * RB, (rb + 1) * RB
        cmask = causal[r0:r1, :r1]
        dm = [jnp.where(cmask, b_c[i][r0:r1] + rowv[i][:, :r1], -jnp.inf) for i in U]
        m_t = [jnp.maximum(inter[i][r0:r1], jnp.max(dm[i], axis=-1, keepdims=True)) for i in U]
        qk = [mm(qh[i][r0:r1], kh[i][:r1], NT) for i in U]
        yield
        s = [qk[i] * jnp.exp(dm[i] - m_t[i]) for i in U]
        w_i = [jnp.exp(inter[i][r0:r1] - m_t[i]) for i in U]
        num = [mm(s[i], vh[i][:r1], NN) + w_i[i] * qc[i][r0:r1] for i in U]
        yield
        for i in U:
            den = num[i][:, HD_M:HD_M + 1]
            hh[i].append(jnp.where(head_lane, num[i], 0.0) / jnp.maximum(jnp.abs(den), jnp.exp(-m_t[i])))
            m_last[i] = m_t[i][RB - 1:RB]
    hh = [_cat_rows(h) for h in hh]
    yield
    outs = []
    for i, (g, h) in enumerate(units):
        b_l = b_c[i][L - 1:L]
        m_new = m_last[i]
        w_s = jnp.exp(b_l - b_c[i] + i_c[i] - m_new)
        w_p = jnp.exp(b_l + m_prev[i] - m_new)
        ks = w_s * kh[i]
        c_scr[g, h] = w_p * c_h[i] + mm(vh[i], ks, TN)
        m_scr[g, h:h + 1, :] = jnp.broadcast_to(m_new, (1, LANES))
        mean = jnp.sum(hh[i], axis=-1, keepdims=True) * (1.0 / HD_M)
        hc = jnp.where(head_lane, hh[i] - mean, 0.0)
        var = jnp.sum(hc * hc, axis=-1, keepdims=True) * (1.0 / HD_M)
        outs.append(hc * lax.rsqrt(var + GN_EPS_M))
    yield

    hn = _cat_rows([jnp.concatenate(outs[g * H_M:(g + 1) * H_M], axis=1) for g in seqs])
    y = (hn * gnw_ref[...] + skip_ref[...] * uc) * _sigmoid(og)
    for g in seqs:
        _store_seq(y_ref, g, lv, packed, y[g * L:(g + 1) * L])

    @pl.when(is_last)
    def _():
        lane_h = lax.broadcasted_iota(jnp.int32, (1, H_M), 1)
        for g in seqs:
            rows = xp_scr[g, 8 + lv - hist:8 + lv, :]
            cbo_ref[g] = jnp.concatenate([rows[:, h * LANES:h * LANES + HD_M] for h in range(H_M)], axis=1)
            no_ref[g] = jnp.concatenate([c_scr[g, h, HD_M:HD_M + 1, :HD_M] for h in range(H_M)], axis=0)
            m_row = jnp.zeros((1, H_M), F32)
            for h in range(H_M):
                co_ref[g, h] = c_scr[g, h, :HD_M, :HD_M]
                m_row = jnp.where(lane_h == h, m_scr[g, h:h + 1, 0:1], m_row)
            mo_ref[g] = m_row

    for g in seqs:
        tail = xp_scr[g, 8 + L - hist:8 + L, :]
        xp_scr[g, 8 - hist:8, :] = tail


N_RWKV_IN, N_RWKV_OUT, N_RWKV_SCRATCH = 14, 3, 2
N_MLSTM_IN, N_MLSTM_OUT = 19, 7


def _chunk_pos():
    c = pl.program_id(1)
    return c, c == pl.num_programs(1) - 1


def _rwkv_body(*refs, **static):
    for _ in _rwkv_steps(*refs, *_chunk_pos(), **static):
        pass


def _mlstm_body(*refs, **static):
    for _ in _mlstm_steps(*refs, *_chunk_pos(), **static):
        pass


def _mixers_body(*refs, rwkv, mlstm):
    it = iter(refs)
    take = lambda n: [next(it) for _ in range(n)]
    r_in, m_in = take(N_RWKV_IN), take(N_MLSTM_IN)
    r_out, m_out = take(N_RWKV_OUT), take(N_MLSTM_OUT)
    r_scr = take(N_RWKV_SCRATCH)
    m_scr = list(it)
    work = [_rwkv_steps(*r_in, *r_out, *r_scr, *_chunk_pos(), **rwkv),
            _mlstm_steps(*m_in, *m_out, *m_scr, *_chunk_pos(), **mlstm)]
    while work:
        for gen in list(work):
            try:
                next(gen)
            except StopIteration:
                work.remove(gen)


def _state_spec(G, shape, l):
    zeros = (0,) * len(shape)
    return pl.BlockSpec((None, G) + tuple(shape), lambda b, c: (l, b) + zeros)


def _param_spec(shape, l):
    zeros = (0,) * len(shape)
    return pl.BlockSpec((None,) + tuple(shape), lambda b, c: (l,) + zeros)


def _mixer_params():
    return pltpu.CompilerParams(dimension_semantics=("parallel", "arbitrary"), vmem_limit_bytes=VMEM_LIMIT)


class _SeqLayout:
    def __init__(self, B, T, chunk, step_rows, max_seqs):
        self.B, self.T = B, T
        self.packed = T < SHORT_SEQ_ROWS
        if self.packed:
            self.tb, self.L, self.nc = T, SHORT_SEQ_ROWS, 1
        else:
            self.tb = self.L = min(T, chunk)
            self.nc = T // self.L
            assert T % self.L == 0
        assert self.L & (self.L - 1) == 0
        self.G = max(1, min(B, step_rows // self.L, max_seqs))
        assert B % self.G == 0
        self.grid = (B // self.G, self.nc)
        self.static = dict(G=self.G, L=self.L, lv=self.tb, packed=self.packed)

    def view(self, a):
        shape = (self.B // self.G, self.G * self.T) if self.packed else (self.B, self.T)
        return a.reshape(shape + (a.shape[-1],))

    def spec(self, width):
        if self.packed:
            return pl.BlockSpec((1, self.G * self.T, width), lambda b, c: (b, 0, 0))
        return pl.BlockSpec((self.G, self.tb, width), lambda b, c: (b, c, 0))

    def out_shape(self, width):
        shape = (self.B // self.G, self.G * self.T) if self.packed else (self.B, self.T)
        return jax.ShapeDtypeStruct(shape + (width,), F32)


def _rwkv_call(pr, st, W, l, lay):
    G, L = lay.G, lay.L
    vec = lambda n: _param_spec((1, n), l)
    lora = _param_spec((LANES, D_RWKV), l)
    states = [_state_spec(G, (1, P_RWKV), l), _state_spec(G, (H_R, HD_R, HD_R), l)]
    names = ("shift", "wkv")
    in_specs = [lay.spec(P_RWKV)] + states + [vec(P_RWKV), vec(D_RWKV), lora, vec(D_RWKV), lora, lora] \
        + [vec(D_RWKV)] * 5
    out_specs = [lay.spec(D_RWKV)] + states
    out_shape = [lay.out_shape(D_RWKV)] + [jax.ShapeDtypeStruct(st[n].shape, F32) for n in names]
    scratch = [pltpu.VMEM((G, 8 + L, P_RWKV), F32), pltpu.VMEM((G, N_PAIR, LANES, LANES), F32)]
    operands = [lay.view(pr), st["shift"], st["wkv"], W["rwkv_mu"], W["rwkv_w0"], W["rwkv_w_up"], W["rwkv_a0"],
                W["rwkv_a_up"], W["rwkv_g_up"], W["rwkv_k_k"], W["rwkv_k_a"], W["rwkv_r_k"], W["rwkv_ln_w"],
                W["rwkv_ln_b"]]
    assert len(in_specs) == N_RWKV_IN == len(operands)
    return in_specs, out_specs, out_shape, scratch, operands, names


def _mlstm_call(pm, pc, st, W, l, lay):
    G, L = lay.G, lay.L
    vec = lambda n: _param_spec((1, n), l)
    sq = _param_spec((D_MPAD, D_MPAD), l)
    gate = _param_spec((D_MPAD, LANES), l)
    states = [_state_spec(G, (MCONV_W - 1, D_MLSTM), l), _state_spec(G, (H_M, HD_M, HD_M), l),
              _state_spec(G, (H_M, HD_M), l), _state_spec(G, (1, H_M), l),
              _state_spec(G, (CONV_W - 1, D_CONV), l)]
    names = ("mconv", "mC", "mn", "mm", "conv")
    in_specs = [lay.spec(2 * D_MPAD), lay.spec(3 * D_CONV)] + states \
        + [_param_spec((MCONV_W, D_MPAD), l), vec(D_MPAD), sq, sq, sq, gate, gate, gate, vec(LANES),
           vec(D_MPAD), vec(D_MPAD), _param_spec((CONV_W, D_CONV), l)]
    out_specs = [lay.spec(D_MPAD), lay.spec(D_CONV)] + states
    out_shape = [lay.out_shape(D_MPAD), lay.out_shape(D_CONV)] + [jax.ShapeDtypeStruct(st[n].shape, F32) for n in names]
    scratch = [pltpu.VMEM((G, 8 + L, D_MPAD), F32), pltpu.VMEM((G, H_M, LANES, LANES), F32),
               pltpu.VMEM((G, H_M, LANES), F32), pltpu.VMEM((G, 8 + L, D_CONV), F32)]
    operands = [lay.view(pm), lay.view(pc)] + [st[n] for n in names] + [
        W["mlstm_conv_w"], W["mlstm_conv_b"], W["mlstm_wq"], W["mlstm_wk"], W["mlstm_wv"], W["mlstm_wgq"],
        W["mlstm_wgk"], W["mlstm_wgv"], W["mlstm_b_gate"], W["mlstm_gn_w"], W["mlstm_skip"], W["conv_w"]]
    assert len(in_specs) == N_MLSTM_IN == len(operands)
    return in_specs, out_specs, out_shape, scratch, operands, names


def _layer_mixers(p, st, W, l, B, T):
    pc, pr, pm = p
    n_tok = B * T
    fused = T < SHORT_SEQ_ROWS
    rk = _SeqLayout(B, T, RWKV_CHUNK, RWKV_STEP_ROWS, FUSED_MAX_SEQS if fused else RWKV_MAX_SEQS)
    ml = _SeqLayout(B, T, MLSTM_CHUNK, MLSTM_STEP_ROWS, FUSED_MAX_SEQS if fused else MLSTM_MAX_SEQS)
    r_in, r_out, r_shape, r_scr, r_ops, r_names = _rwkv_call(pr, st, W, l, rk)
    m_in, m_out, m_shape, m_scr, m_ops, m_names = _mlstm_call(pm, pc, st, W, l, ml)
    r_alias = {1: 1, 2: 2}
    m_alias = {2 + k: 2 + k for k in range(len(m_names))}
    if fused:
        assert rk.grid == ml.grid
        aliases = dict(r_alias)
        aliases.update({N_RWKV_IN + i: N_RWKV_OUT + o for i, o in m_alias.items()})
        outs = pl.pallas_call(
            functools.partial(_mixers_body, rwkv=rk.static, mlstm=ml.static), grid=rk.grid,
            in_specs=r_in + m_in, out_specs=r_out + m_out, out_shape=r_shape + m_shape,
            input_output_aliases=aliases, scratch_shapes=r_scr + m_scr,
            compiler_params=_mixer_params(), name="mixers")(*r_ops, *m_ops)
        r_res, m_res = outs[:N_RWKV_OUT], outs[N_RWKV_OUT:]
    else:
        r_res = pl.pallas_call(
            functools.partial(_rwkv_body, **rk.static), grid=rk.grid, in_specs=r_in, out_specs=r_out,
            out_shape=r_shape, input_output_aliases=r_alias, scratch_shapes=r_scr,
            compiler_params=_mixer_params(), name="rwkv7")(*r_ops)
        m_res = pl.pallas_call(
            functools.partial(_mlstm_body, **ml.static), grid=ml.grid, in_specs=m_in, out_specs=m_out,
            out_shape=m_shape, input_output_aliases=m_alias, scratch_shapes=m_scr,
            compiler_params=_mixer_params(), name="mlstm_conv")(*m_ops)
    for n, o in zip(r_names, r_res[1:]):
        st[n] = o
    for n, o in zip(m_names, m_res[2:]):
        st[n] = o
    return m_res[1].reshape(n_tok, D_CONV), r_res[0].reshape(n_tok, D_RWKV), m_res[0].reshape(n_tok, D_MPAD)


def _pad_heads(x, axis):
    axis = axis % x.ndim
    shp = x.shape
    x = x.reshape(shp[:axis] + (H_M, HD_M) + shp[axis + 1:])
    pad = [(0, 0)] * x.ndim
    pad[axis + 1] = (0, LANES - HD_M)
    x = jnp.pad(x, pad)
    return x.reshape(shp[:axis] + (D_MPAD,) + shp[axis + 1:])


def _head_block_diag(w):
    wp = jnp.pad(w, ((0, 0), (0, 0), (0, LANES - HD_M), (0, LANES - HD_M)))
    eye = jnp.eye(H_M, dtype=w.dtype)
    return jnp.einsum("lhde,hg->lhdge", wp, eye).reshape(w.shape[0], D_MPAD, D_MPAD)


def _prep_weights(norm_ff, ffn_gate, ffn_up, ffn_down, norm_mix, w_in, w_out, conv_w,
                  rwkv_mu, rwkv_w0, rwkv_w_up, rwkv_a0, rwkv_a_up, rwkv_g_up, rwkv_k_k, rwkv_k_a, rwkv_r_k,
                  rwkv_ln_w, rwkv_ln_b, mlstm_conv_w, mlstm_conv_b, mlstm_wq, mlstm_wk, mlstm_wv,
                  mlstm_w_gate, mlstm_b_gate, mlstm_gn_w, mlstm_skip, norm_final):
    depth = w_in.shape[0]
    pc, pr = 3 * D_CONV, P_RWKV
    row = lambda x: x[:, None, :]
    W = {}
    W["norm_ff"] = norm_ff[:, :, None, :]
    W["ffn_gate"] = ffn_gate.astype(BF16)
    W["ffn_up"] = ffn_up.astype(BF16)
    W["ffn_down"] = ffn_down.astype(BF16)
    W["norm_mix"] = row(norm_mix)
    W["w_in_c"] = w_in[:, :, :pc].astype(BF16)
    W["w_in_r"] = w_in[:, :, pc:pc + pr].astype(BF16)
    w_u = _pad_heads(w_in[:, :, pc + pr:pc + pr + D_MLSTM], 2)
    w_o = _pad_heads(w_in[:, :, pc + pr + D_MLSTM:], 2)
    W["w_in_m"] = jnp.concatenate([w_u, w_o], axis=2).astype(BF16)
    W["w_out_c"] = w_out[:, :D_CONV].astype(BF16)
    W["w_out_r"] = w_out[:, D_CONV:D_CONV + D_RWKV].astype(BF16)
    W["w_out_m"] = _pad_heads(w_out[:, D_CONV + D_RWKV:], 1).astype(BF16)
    W["conv_w"] = conv_w
    W["rwkv_mu"] = row(rwkv_mu)
    W["rwkv_w0"] = row(rwkv_w0)
    W["rwkv_a0"] = row(rwkv_a0)
    zl = lambda n: jnp.zeros((depth, n, D_RWKV), F32)
    W["rwkv_w_up"] = jnp.concatenate([rwkv_w_up, zl(LANES - LORA_W)], axis=1)
    W["rwkv_a_up"] = jnp.concatenate([zl(LORA_W), rwkv_a_up, zl(LORA_G)], axis=1)
    W["rwkv_g_up"] = jnp.concatenate([zl(LORA_W + LORA_A), rwkv_g_up], axis=1)
    W["rwkv_k_k"] = row(rwkv_k_k)
    W["rwkv_k_a"] = row(rwkv_k_a)
    W["rwkv_r_k"] = row(rwkv_r_k.reshape(depth, D_RWKV))
    W["rwkv_ln_w"] = row(rwkv_ln_w)
    W["rwkv_ln_b"] = row(rwkv_ln_b)
    W["mlstm_conv_w"] = _pad_heads(mlstm_conv_w, 2)
    W["mlstm_conv_b"] = row(_pad_heads(mlstm_conv_b, 1))
    W["mlstm_wq"] = _head_block_diag(mlstm_wq)
    W["mlstm_wk"] = _head_block_diag(mlstm_wk)
    W["mlstm_wv"] = _head_block_diag(mlstm_wv)
    wg = mlstm_w_gate.reshape(depth, H_M, 3, HD_M, 2 * H_M)
    for i, name in enumerate(("mlstm_wgq", "mlstm_wgk", "mlstm_wgv")):
        part = jnp.pad(wg[:, :, i], ((0, 0), (0, 0), (0, LANES - HD_M), (0, LANES - 2 * H_M)))
        W[name] = part.reshape(depth, D_MPAD, LANES)
    W["mlstm_b_gate"] = row(jnp.pad(mlstm_b_gate, ((0, 0), (0, LANES - 2 * H_M))))
    W["mlstm_gn_w"] = row(_pad_heads(mlstm_gn_w, 1))
    W["mlstm_skip"] = row(_pad_heads(mlstm_skip, 1))
    W["norm_final"] = norm_final[None, :]
    return W


_STATE_NAMES = ("conv", "shift", "wkv", "mconv", "mC", "mn", "mm")


def _run_trunks(x_p, states_p, x_s, states_s, W):
    (Bp, Tp, D), (Bs, Ts, _) = x_p.shape, x_s.shape
    depth = W["w_in_c"].shape[0]
    st_p, st_s = dict(zip(_STATE_NAMES, states_p)), dict(zip(_STATE_NAMES, states_s))
    for st in (st_p, st_s):
        st["mm"] = st["mm"][:, :, None, :]
    xp, xs = x_p.reshape(Bp * Tp, D), x_s.reshape(Bs * Ts, D)
    for l in range(depth):
        xp, xs = _ffn(xp, xs, W, l, 0)
        p_p, p_s = _inproj(xp, xs, W, l)
        y_p = _layer_mixers(p_p, st_p, W, l, Bp, Tp)
        y_s = _layer_mixers(p_s, st_s, W, l, Bs, Ts)
        xp, xs = _mix_ffn(xp, y_p, xs, y_s, W, l, final=(l == depth - 1))
    for st in (st_p, st_s):
        st["mm"] = st["mm"][:, :, 0, :]
    return ((xp.reshape(Bp, Tp, D), xs.reshape(Bs, Ts, D))
            + tuple(st_p[n] for n in _STATE_NAMES) + tuple(st_s[n] for n in _STATE_NAMES))


def kernel(x_prompt, x_sample, cache_conv, cache_shift, state_wkv, cache_mconv, state_mC, state_mn, state_mm, norm_ff, ffn_gate, ffn_up, ffn_down, norm_mix, w_in, w_out, conv_w, rwkv_mu, rwkv_w0, rwkv_w_up, rwkv_a0, rwkv_a_up, rwkv_g_up, rwkv_k_k, rwkv_k_a, rwkv_r_k, rwkv_ln_w, rwkv_ln_b, mlstm_conv_w, mlstm_conv_b, mlstm_wq, mlstm_wk, mlstm_wv, mlstm_w_gate, mlstm_b_gate, mlstm_gn_w, mlstm_skip, norm_final):
    W = _prep_weights(norm_ff, ffn_gate, ffn_up, ffn_down, norm_mix, w_in, w_out, conv_w,
                      rwkv_mu, rwkv_w0, rwkv_w_up, rwkv_a0, rwkv_a_up, rwkv_g_up, rwkv_k_k, rwkv_k_a, rwkv_r_k,
                      rwkv_ln_w, rwkv_ln_b, mlstm_conv_w, mlstm_conv_b, mlstm_wq, mlstm_wk, mlstm_wv,
                      mlstm_w_gate, mlstm_b_gate, mlstm_gn_w, mlstm_skip, norm_final)
    depth = w_in.shape[0]
    Bp = x_prompt.shape[0]
    zeros = lambda *shape: jnp.zeros((depth, Bp) + shape, F32)
    st_p = (zeros(CONV_W - 1, D_CONV), zeros(1, P_RWKV), zeros(H_R, HD_R, HD_R), zeros(MCONV_W - 1, D_MLSTM),
            zeros(H_M, HD_M, HD_M), zeros(H_M, HD_M), zeros(H_M))
    st_s = (cache_conv, cache_shift, state_wkv, cache_mconv, state_mC, state_mn, state_mm)
    return _run_trunks(x_prompt, st_p, x_sample, st_s, W)
```

```python
import functools
import math

import jax
import jax.numpy as jnp
from jax import lax
from jax.experimental import pallas as pl
from jax.experimental.pallas import tpu as pltpu

F32 = jnp.float32
BF16 = jnp.bfloat16

LANES = 128
H_R, HD_R = 6, 64
D_RWKV = H_R * HD_R
N_PAIR = H_R // 2
LORA_W, LORA_A, LORA_G = 32, 32, 64
P_RWKV = 3 * D_RWKV + LORA_W + LORA_A + LORA_G
H_M, HD_M = 4, 96
D_MLSTM = H_M * HD_M
D_MPAD = H_M * LANES
D_CONV = 256
CONV_W, MCONV_W = 3, 4
RMS_EPS = 1e-6
GN_EPS_R = 64e-5
GN_EPS_M = 1e-6
SHORT_SEQ_ROWS = 8
RWKV_CHUNK = 64
RWKV_STEP_ROWS = 256
MLSTM_CHUNK = 256
MLSTM_STEP_ROWS = 256
MLSTM_ROW_BLOCK = 128
RWKV_MAX_SEQS = 16
MLSTM_MAX_SEQS = 8
FUSED_MAX_SEQS = 8
TOKEN_TILE = 512
FFN_COLS = 256
VMEM_LIMIT = 52 * 1024 * 1024

NN = ((1,), (0,))
NT = ((1,), (1,))
TN = ((0,), (0,))

MODE_PROJ = "b1"
MODE_SCAN = "b1"


def _mm(a, b, dims=NN, mode="b1"):
    dn = (dims, ((), ()))
    dot = lambda x, y: lax.dot_general(x, y, dn, preferred_element_type=F32)
    ah = a.astype(BF16)
    bh = b.astype(BF16)
    if mode == "b1":
        return dot(ah, bh)
    al = (a - ah.astype(F32)).astype(BF16)
    bl = (b - bh.astype(F32)).astype(BF16)
    return dot(ah, bh) + (dot(ah, bl) + dot(al, bh))


def _rms(x, g):
    return x * lax.rsqrt(jnp.mean(x * x, axis=-1, keepdims=True) + RMS_EPS) * g


def _sigmoid(x):
    return 1.0 / (1.0 + jnp.exp(-x))


def _softplus(x):
    return jnp.maximum(x, 0.0) + jnp.log1p(jnp.exp(-jnp.abs(x)))


def _masked_sum(mask, x):
    m = mask.astype(BF16)
    x1 = x.astype(BF16)
    r1 = x - x1.astype(F32)
    x2 = r1.astype(BF16)
    x3 = (r1 - x2.astype(F32)).astype(BF16)
    dot = lambda y: jnp.dot(m, y, preferred_element_type=F32)
    return dot(x1) + (dot(x2) + dot(x3))


def _seq_rows(ref, g, L, tb, packed):
    x = ref[0, g * tb:(g + 1) * tb, :] if packed else ref[g]
    if tb < L:
        x = jnp.concatenate([x, jnp.zeros((L - tb, x.shape[1]), x.dtype)], axis=0)
    return x


def _store_seq(ref, g, tb, packed, rows):
    if packed:
        ref[0, g * tb:(g + 1) * tb, :] = rows[:tb]
    else:
        ref[g] = rows[:tb]


def _cat_rows(xs):
    return xs[0] if len(xs) == 1 else jnp.concatenate(xs, axis=0)


def _half_ffn(x, g_ref, wg_ref, wu_ref, wd_ref):
    n = _rms(x, g_ref[...]).astype(BF16)
    acc = jnp.zeros(x.shape, F32)
    for c in range(wg_ref.shape[1] // FFN_COLS):
        sl = slice(c * FFN_COLS, (c + 1) * FFN_COLS)
        hg = jnp.dot(n, wg_ref[:, sl], preferred_element_type=F32)
        hu = jnp.dot(n, wu_ref[:, sl], preferred_element_type=F32)
        h = (hg * _sigmoid(hg) * hu).astype(BF16)
        acc = acc + jnp.dot(h, wd_ref[sl, :], preferred_element_type=F32)
    return x + 0.5 * acc


def _ffn_body(x_ref, g_ref, wg_ref, wu_ref, wd_ref, o_ref):
    o_ref[...] = _half_ffn(x_ref[...], g_ref, wg_ref, wu_ref, wd_ref)


def _mix_ffn_body(x_ref, yc_ref, yr_ref, ym_ref, wc_ref, wr_ref, wm_ref, g_ref, wg_ref, wu_ref, wd_ref,
                  gf_ref, o_ref, *, final):
    acc = jnp.dot(yc_ref[...].astype(BF16), wc_ref[...], preferred_element_type=F32)
    acc = acc + jnp.dot(yr_ref[...].astype(BF16), wr_ref[...], preferred_element_type=F32)
    acc = acc + jnp.dot(ym_ref[...].astype(BF16), wm_ref[...], preferred_element_type=F32)
    x = _half_ffn(x_ref[...] + acc, g_ref, wg_ref, wu_ref, wd_ref)
    o_ref[...] = _rms(x, gf_ref[...]) if final else x


def _inproj_body(x_ref, g_ref, wc_ref, wr_ref, wm_ref, pc_ref, pr_ref, pm_ref):
    n = _rms(x_ref[...], g_ref[...]).astype(BF16)
    pc_ref[...] = jnp.dot(n, wc_ref[...], preferred_element_type=F32)
    pr_ref[...] = jnp.dot(n, wr_ref[...], preferred_element_type=F32)
    pm_ref[...] = jnp.dot(n, wm_ref[...], preferred_element_type=F32)


def _resident(shape, *lead):
    nlead = len(lead)
    zeros = (0,) * len(shape)
    return pl.BlockSpec((None,) * nlead + tuple(shape), lambda i: tuple(lead) + zeros,
                        pipeline_mode=pl.Buffered(1))


def _two_trunk_body(*refs, inner, n_x, n_par, n_out, tiles):
    xs_p, xs_s = refs[:n_x], refs[n_x:2 * n_x]
    params = refs[2 * n_x:2 * n_x + n_par]
    outs = refs[2 * n_x + n_par:]
    i = pl.program_id(0)

    @pl.when(i < tiles)
    def _():
        inner(*xs_p, *params, *outs[:n_out])

    @pl.when(i == tiles)
    def _():
        inner(*xs_s, *params, *outs[n_out:])


def _token_call(inner, xs_p, xs_s, params, param_specs, out_widths, name):
    n_p, n_s = xs_p[0].shape[0], xs_s[0].shape[0]
    tm = min(TOKEN_TILE, n_p)
    tiles = n_p // tm
    assert n_p % tm == 0
    rows_p = lambda w: pl.BlockSpec((tm, w), lambda i: (jnp.minimum(i, tiles - 1), 0))
    rows_s_in = lambda w: pl.BlockSpec((n_s, w), lambda i: (0, 0), pipeline_mode=pl.Buffered(1))
    rows_s_out = lambda w: pl.BlockSpec((n_s, w), lambda i: (0, 0))
    outs = pl.pallas_call(
        functools.partial(_two_trunk_body, inner=inner, n_x=len(xs_p), n_par=len(params),
                          n_out=len(out_widths), tiles=tiles),
        grid=(tiles + 1,),
        in_specs=[rows_p(a.shape[1]) for a in xs_p] + [rows_s_in(a.shape[1]) for a in xs_s] + list(param_specs),
        out_specs=[rows_p(w) for w in out_widths] + [rows_s_out(w) for w in out_widths],
        out_shape=[jax.ShapeDtypeStruct((n_p, w), F32) for w in out_widths]
                  + [jax.ShapeDtypeStruct((n_s, w), F32) for w in out_widths],
        compiler_params=pltpu.CompilerParams(dimension_semantics=("arbitrary",), vmem_limit_bytes=VMEM_LIMIT),
        name=name,
    )(*xs_p, *xs_s, *params)
    k = len(out_widths)
    return outs[:k], outs[k:]


def _ffn(xp, xs, W, l, which):
    d = xp.shape[1]
    f = W["ffn_gate"].shape[-1]
    specs = [_resident((1, d), l, which), _resident((d, f), l, which), _resident((d, f), l, which),
             _resident((f, d), l, which)]
    (op,), (os_,) = _token_call(_ffn_body, [xp], [xs], [W["norm_ff"], W["ffn_gate"], W["ffn_up"], W["ffn_down"]],
                                specs, [d], "ffn")
    return op, os_


def _inproj(xp, xs, W, l):
    d = xp.shape[1]
    ws = [W["w_in_c"], W["w_in_r"], W["w_in_m"]]
    widths = [w.shape[-1] for w in ws]
    specs = [_resident((1, d), l)] + [_resident((d, w), l) for w in widths]
    return _token_call(_inproj_body, [xp], [xs], [W["norm_mix"]] + ws, specs, widths, "inproj")


def _mix_ffn(xp, yp, xs, ys, W, l, final):
    d = xp.shape[1]
    f = W["ffn_gate"].shape[-1]
    ws = [W["w_out_c"], W["w_out_r"], W["w_out_m"]]
    specs = ([_resident(w.shape[1:], l) for w in ws]
             + [_resident((1, d), l, 1), _resident((d, f), l, 1), _resident((d, f), l, 1),
                _resident((f, d), l, 1), _resident((1, d))])
    params = ws + [W["norm_ff"], W["ffn_gate"], W["ffn_up"], W["ffn_down"], W["norm_final"]]
    (op,), (os_,) = _token_call(functools.partial(_mix_ffn_body, final=final), [xp] + list(yp), [xs] + list(ys),
                                params, specs, [d], "mix_ffn")
    return op, os_


def _short_conv(pc_ref, buf_ref, w_ref, y_ref, nb_ref, xp_scr, c, is_last, *, G, L, lv, packed):
    hist = CONV_W - 1
    w = w_ref[...]
    for g in range(G):
        @pl.when(c == 0)
        def _():
            xp_scr[g, 8 - hist:8, :] = buf_ref[g]

        p = _seq_rows(pc_ref, g, L, lv, packed)
        b_gate = p[:, :D_CONV]
        xp_scr[g, 8:8 + L, :] = p[:, D_CONV:2 * D_CONV] * p[:, 2 * D_CONV:]
        z = w[0:1] * xp_scr[g, 6:6 + L, :]
        z = z + w[1:2] * xp_scr[g, 7:7 + L, :]
        z = z + w[2:3] * xp_scr[g, 8:8 + L, :]
        _store_seq(y_ref, g, lv, packed, b_gate * z)

        @pl.when(is_last)
        def _():
            nb_ref[g] = xp_scr[g, 8 + lv - hist:8 + lv, :]

        tail = xp_scr[g, 8 + L - hist:8 + L, :]
        xp_scr[g, 8 - hist:8, :] = tail


def _rwkv_steps(pr_ref, sh_ref, s0_ref, mu_ref, w0_ref, wup_ref, a0_ref, aup_ref, gup_ref, kk_ref, ka_ref,
                rk_ref, lnw_ref, lnb_ref, y_ref, sho_ref, so_ref, xp_scr, s_scr, c, is_last, *, G, L, lv, packed):
    GL = G * L
    shift = L.bit_length() - 1
    seqs = range(G)

    @pl.when(c == 0)
    def _():
        for g in seqs:
            xp_scr[g, 7:8, :] = sh_ref[g]
            for j in range(N_PAIR):
                s_scr[g, j] = jnp.zeros((LANES, LANES), F32)
                s_scr[g, j, :HD_R, :HD_R] = s0_ref[g, 2 * j]
                s_scr[g, j, HD_R:, HD_R:] = s0_ref[g, 2 * j + 1]

    p_seq = [_seq_rows(pr_ref, g, L, lv, packed) for g in seqs]
    for g in seqs:
        xp_scr[g, 8:8 + L, :] = p_seq[g]
    p = _cat_rows(p_seq)
    prev = _cat_rows([xp_scr[g, 7:7 + L, :] for g in seqs])
    ps = p + (prev - p) * mu_ref[...]
    r = ps[:, 0:D_RWKV]
    k = ps[:, D_RWKV:2 * D_RWKV]
    v = ps[:, 2 * D_RWKV:3 * D_RWKV]
    x4 = ps[:, 3 * D_RWKV:]
    lw = -math.exp(-0.5) * _sigmoid(w0_ref[...] + _mm(jnp.tanh(x4), wup_ref[...], NN, MODE_PROJ))
    a = _sigmoid(a0_ref[...] + _mm(x4, aup_ref[...], NN, MODE_PROJ))
    gate = _mm(_sigmoid(x4), gup_ref[...], NN, MODE_PROJ)
    yield

    lane = lax.broadcasted_iota(jnp.int32, (1, LANES), 1)
    lo = lane < HD_R

    def pair_sum(x):
        s0 = jnp.sum(jnp.where(lo, x, 0.0), axis=-1, keepdims=True)
        s1 = jnp.sum(jnp.where(lo, 0.0, x), axis=-1, keepdims=True)
        return jnp.where(lo, s0, s1)

    def head_sum(x):
        return jnp.concatenate([pair_sum(x[:, j * LANES:(j + 1) * LANES]) for j in range(N_PAIR)], axis=1)

    kk = k * kk_ref[...]
    kk = kk * lax.rsqrt(jnp.maximum(head_sum(kk * kk), 1e-24))
    k2 = k * (1.0 + (a - 1.0) * ka_ref[...])
    bonus = head_sum(r * k2 * rk_ref[...]) * v

    if lv < L:
        valid = (lax.broadcasted_iota(jnp.int32, (GL, 1), 0) & (L - 1)) < lv
        lw = jnp.where(valid, lw, 0.0)
        kk = jnp.where(valid, kk, 0.0)
        k2 = jnp.where(valid, k2, 0.0)

    ri = lax.broadcasted_iota(jnp.int32, (GL, GL), 0)
    ci = lax.broadcasted_iota(jnp.int32, (GL, GL), 1)
    tri = jnp.logical_and(ci <= ri, (ci >> shift) == (ri >> shift))
    cl = _masked_sum(tri, lw)
    e_pos = jnp.exp(cl)
    e_neg = jnp.exp(-cl)
    a_t = -kk * jnp.exp(cl - lw)
    b_t = kk * a * e_neg
    k_t = k2 * e_neg
    r_t = r * e_pos
    yield

    L2 = 2 * L
    ri2 = lax.broadcasted_iota(jnp.int32, (L2, L2), 0)
    ci2 = lax.broadcasted_iota(jnp.int32, (L2, L2), 1)
    tri2 = ri2 & (L - 1)
    tci2 = ci2 & (L - 1)
    strict = tci2 < tri2
    incl = tci2 <= tri2
    eye = (ri2 == ci2).astype(F32)
    row_lo = lax.broadcasted_iota(jnp.int32, (L2, 1), 0) < L
    own = jnp.logical_and(row_lo, lo) | jnp.logical_and(jnp.logical_not(row_lo), jnp.logical_not(lo))

    def stack(x):
        return jnp.concatenate([jnp.where(lo, x, 0.0), jnp.where(lo, 0.0, x)], axis=0)

    units = [(g, j) for g in seqs for j in range(N_PAIR)]
    U = range(len(units))
    rsl = [slice(g * L, (g + 1) * L) for g, _ in units]
    lsl = [slice(j * LANES, (j + 1) * LANES) for _, j in units]
    mm = functools.partial(_mm, mode=MODE_SCAN)
    xa, xb, xk, xr, xv = ([stack(t[rsl[i], lsl[i]]) for i in U] for t in (a_t, b_t, k_t, r_t, v))
    s_bd = [s_scr[g, j] for g, j in units]
    xar = [jnp.concatenate([xa[i], xr[i]], axis=0) for i in U]
    gram = [mm(xar[i], jnp.concatenate([xb[i], xk[i]], axis=0), NT) for i in U]
    a_ab = [jnp.where(strict, gm[:L2, :L2], 0.0) for gm in gram]
    a_ak = [jnp.where(strict, gm[:L2, L2:], 0.0) for gm in gram]
    m_rb = [jnp.where(incl, gm[L2:, :L2], 0.0) for gm in gram]
    m_rk = [jnp.where(incl, gm[L2:, L2:], 0.0) for gm in gram]
    yield
    sh = [mm(xar[i], s_bd[i], NT) for i in U]
    akv = [sh[i] + mm(jnp.concatenate([a_ak[i], m_rk[i]], axis=0), xv[i], NN) for i in U]
    yield
    tinv = [eye + m for m in a_ab]
    steps = L.bit_length() - 2
    if steps > 0:
        pw = [mm(m, m, NN) for m in a_ab]
    for step in range(steps):
        if step < steps - 1:
            both = [mm(pw[i], jnp.concatenate([tinv[i], pw[i]], axis=1), NN) for i in U]
            tinv = [tinv[i] + both[i][:, :L2] for i in U]
            pw = [both[i][:, L2:] for i in U]
        else:
            tinv = [tinv[i] + mm(pw[i], tinv[i], NN) for i in U]
    u = [mm(tinv[i], akv[i][:L2], NN) for i in U]
    yield
    o = [akv[i][L2:] + mm(m_rb[i], u[i], NN) for i in U]
    yield
    for i, (g, j) in enumerate(units):
        g_last = e_pos[g * L + L - 1:g * L + L, lsl[i]]
        upd = mm(jnp.concatenate([u[i], xv[i]], axis=0), jnp.concatenate([xb[i], xk[i]], axis=0), TN)
        s_scr[g, j] = (s_bd[i] + upd) * g_last
    yield
    outs = []
    for i in U:
        mean = jnp.sum(o[i], axis=-1, keepdims=True) * (1.0 / HD_R)
        oc = jnp.where(own, o[i] - mean, 0.0)
        var = jnp.sum(oc * oc, axis=-1, keepdims=True) * (1.0 / HD_R)
        on = oc * lax.rsqrt(var + GN_EPS_R)
        outs.append(on[:L] + on[L:])

    yn = _cat_rows([jnp.concatenate(outs[g * N_PAIR:(g + 1) * N_PAIR], axis=1) for g in seqs])
    y = (yn * lnw_ref[...] + lnb_ref[...] + bonus) * gate
    for g in seqs:
        _store_seq(y_ref, g, lv, packed, y[g * L:(g + 1) * L])

    @pl.when(is_last)
    def _():
        for g in seqs:
            sho_ref[g] = xp_scr[g, 8 + lv - 1:8 + lv, :]
            for j in range(N_PAIR):
                so_ref[g, 2 * j] = s_scr[g, j, :HD_R, :HD_R]
                so_ref[g, 2 * j + 1] = s_scr[g, j, HD_R:, HD_R:]

    for g in seqs:
        xp_scr[g, 7:8, :] = xp_scr[g, 7 + L:8 + L, :]


def _mlstm_steps(pm_ref, pc_ref, cb_ref, c0_ref, n0_ref, m0_ref, sc_ref, cw_ref, cbias_ref, wq_ref, wk_ref, wv_ref,
                 wgq_ref, wgk_ref, wgv_ref, bg_ref, gnw_ref, skip_ref, scw_ref,
                 y_ref, yc_ref, cbo_ref, co_ref, no_ref, mo_ref, sco_ref, xp_scr, c_scr, m_scr, xpc_scr,
                 c, is_last, *, G, L, lv, packed):
    _short_conv(pc_ref, sc_ref, scw_ref, yc_ref, sco_ref, xpc_scr, c, is_last, G=G, L=L, lv=lv, packed=packed)
    yield
    hist = MCONV_W - 1
    GL = G * L
    shift = L.bit_length() - 1
    seqs = range(G)

    @pl.when(c == 0)
    def _():
        for g in seqs:
            xp_scr[g, 0:8, :] = jnp.zeros((8, D_MPAD), F32)
            cb = cb_ref[g]
            m0 = m0_ref[g]
            n0 = n0_ref[g]
            for h in range(H_M):
                xp_scr[g, 8 - hist:8, h * LANES:h * LANES + HD_M] = cb[:, h * HD_M:(h + 1) * HD_M]
                c_scr[g, h] = jnp.zeros((LANES, LANES), F32)
                c_scr[g, h, :HD_M, :HD_M] = c0_ref[g, h]
                c_scr[g, h, HD_M:HD_M + 1, :HD_M] = n0[h:h + 1, :]
                m_scr[g, h:h + 1, :] = jnp.broadcast_to(m0[:, h:h + 1], (1, LANES))

    cw = cw_ref[...]
    pm_seq = [_seq_rows(pm_ref, g, L, lv, packed) for g in seqs]
    ucs = []
    for g in seqs:
        u_g = pm_seq[g][:, :D_MPAD]
        xp_scr[g, 8:8 + L, :] = u_g
        acc = cw[0:1] * xp_scr[g, 5:5 + L, :]
        acc = acc + cw[1:2] * xp_scr[g, 6:6 + L, :]
        acc = acc + cw[2:3] * xp_scr[g, 7:7 + L, :]
        ucs.append(acc + cw[3:4] * u_g)
    pm = _cat_rows(pm_seq)
    u = pm[:, :D_MPAD]
    og = pm[:, D_MPAD:]
    uc = _cat_rows(ucs) + cbias_ref[...]
    uc = uc * _sigmoid(uc)
    yield
    q = _mm(uc, wq_ref[...], NN, MODE_PROJ)
    k = _mm(uc, wk_ref[...], NN, MODE_PROJ)
    v = _mm(u, wv_ref[...], NN, MODE_PROJ)
    gp = (_mm(q, wgq_ref[...], NN, MODE_PROJ) + _mm(k, wgk_ref[...], NN, MODE_PROJ)
          + _mm(v, wgv_ref[...], NN, MODE_PROJ)) + bg_ref[...]
    k = k * (HD_M ** -0.5)
    yield

    ig = gp
    lf = -_softplus(-gp)
    if lv < L:
        valid = (lax.broadcasted_iota(jnp.int32, (GL, 1), 0) & (L - 1)) < lv
        ig = jnp.where(valid, ig, -jnp.inf)
        lf = jnp.where(valid, lf, 0.0)
    rg = lax.broadcasted_iota(jnp.int32, (GL, GL), 0)
    cg = lax.broadcasted_iota(jnp.int32, (GL, GL), 1)
    tri = jnp.logical_and(cg <= rg, (cg >> shift) == (rg >> shift))
    bcum = _masked_sum(tri, lf)
    lane = lax.broadcasted_iota(jnp.int32, (1, LANES), 1)
    gates = jnp.where(lane < H_M, ig, bcum)
    gates_t = gates.T
    yield
    head_lane = lane < HD_M
    one_lane = lane == HD_M
    ri = lax.broadcasted_iota(jnp.int32, (L, L), 0)
    ci = lax.broadcasted_iota(jnp.int32, (L, L), 1)
    causal = ci <= ri

    units = [(g, h) for g in seqs for h in range(H_M)]
    U = range(len(units))
    mm = functools.partial(_mm, mode=MODE_SCAN)
    rsl = [slice(g * L, (g + 1) * L) for g, _ in units]
    lsl = [slice(h * LANES, (h + 1) * LANES) for _, h in units]
    qh = [q[rsl[i], lsl[i]] for i in U]
    kh = [k[rsl[i], lsl[i]] for i in U]
    vh = [jnp.where(one_lane, 1.0, v[rsl[i], lsl[i]]) for i in U]
    i_c = [gates[rsl[i], h:h + 1] for i, (g, h) in enumerate(units)]
    b_c = [gates[rsl[i], H_M + h:H_M + h + 1] for i, (g, h) in enumerate(units)]
    i_r = [gates_t[h:h + 1, rsl[i]] for i, (g, h) in enumerate(units)]
    b_r = [gates_t[H_M + h:H_M + h + 1, rsl[i]] for i, (g, h) in enumerate(units)]
    m_prev = [m_scr[g, h:h + 1, 0:1] for g, h in units]
    c_h = [c_scr[g, h] for g, h in units]
    qc = [mm(qh[i], c_h[i], NT) for i in U]
    rowv = [i_r[i] - b_r[i] for i in U]
    inter = [b_c[i] + m_prev[i] for i in U]
    yield
    RB = min(L, MLSTM_ROW_BLOCK)
    hh = [[] for _ in U]
    m_last = [None for _ in U]
    for rb in range(L // RB):
        r0, r1 = rb * RB, (rb + 1) * RB
        cmask = causal[r0:r1, :r1]
        dm = [jnp.where(cmask, b_c[i][r0:r1] + rowv[i][:, :r1], -jnp.inf) for i in U]
        m_t = [jnp.maximum(inter[i][r0:r1], jnp.max(dm[i], axis=-1, keepdims=True)) for i in U]
        qk = [mm(qh[i][r0:r1], kh[i][:r1], NT) for i in U]
        s = [qk[i] * jnp.exp(dm[i] - m_t[i]) for i in U]
        w_i = [jnp.exp(inter[i][r0:r1] - m_t[i]) for i in U]
        num = [mm(s[i], vh[i][:r1], NN) + w_i[i] * qc[i][r0:r1] for i in U]
        yield
        for i in U:
            den = num[i][:, HD_M:HD_M + 1]
            hh[i].append(jnp.where(head_lane, num[i], 0.0) / jnp.maximum(jnp.abs(den), jnp.exp(-m_t[i])))
            m_last[i] = m_t[i][RB - 1:RB]
    hh = [_cat_rows(h) for h in hh]
    yield
    outs = []
    for i, (g, h) in enumerate(units):
        b_l = b_c[i][L - 1:L]
        m_new = m_last[i]
        w_s = jnp.exp(b_l - b_c[i] + i_c[i] - m_new)
        w_p = jnp.exp(b_l + m_prev[i] - m_new)
        ks = w_s * kh[i]
        c_scr[g, h] = w_p * c_h[i] + mm(vh[i], ks, TN)
        m_scr[g, h:h + 1, :] = jnp.broadcast_to(m_new, (1, LANES))
        mean = jnp.sum(hh[i], axis=-1, keepdims=True) * (1.0 / HD_M)
        hc = jnp.where(head_lane, hh[i] - mean, 0.0)
        var = jnp.sum(hc * hc, axis=-1, keepdims=True) * (1.0 / HD_M)
        outs.append(hc * lax.rsqrt(var + GN_EPS_M))
    yield

    hn = _cat_rows([jnp.concatenate(outs[g * H_M:(g + 1) * H_M], axis=1) for g in seqs])
    y = (hn * gnw_ref[...] + skip_ref[...] * uc) * _sigmoid(og)
    for g in seqs:
        _store_seq(y_ref, g, lv, packed, y[g * L:(g + 1) * L])

    @pl.when(is_last)
    def _():
        lane_h = lax.broadcasted_iota(jnp.int32, (1, H_M), 1)
        for g in seqs:
            rows = xp_scr[g, 8 + lv - hist:8 + lv, :]
            cbo_ref[g] = jnp.concatenate([rows[:, h * LANES:h * LANES + HD_M] for h in range(H_M)], axis=1)
            no_ref[g] = jnp.concatenate([c_scr[g, h, HD_M:HD_M + 1, :HD_M] for h in range(H_M)], axis=0)
            m_row = jnp.zeros((1, H_M), F32)
            for h in range(H_M):
                co_ref[g, h] = c_scr[g, h, :HD_M, :HD_M]
                m_row = jnp.where(lane_h == h, m_scr[g, h:h + 1, 0:1], m_row)
            mo_ref[g] = m_row

    for g in seqs:
        tail = xp_scr[g, 8 + L - hist:8 + L, :]
        xp_scr[g, 8 - hist:8, :] = tail


N_RWKV_IN, N_RWKV_OUT, N_RWKV_SCRATCH = 14, 3, 2
N_MLSTM_IN, N_MLSTM_OUT = 19, 7


def _chunk_pos():
    c = pl.program_id(1)
    return c, c == pl.num_programs(1) - 1


def _rwkv_body(*refs, **static):
    for _ in _rwkv_steps(*refs, *_chunk_pos(), **static):
        pass


def _mlstm_body(*refs, **static):
    for _ in _mlstm_steps(*refs, *_chunk_pos(), **static):
        pass


def _mixers_body(*refs, rwkv, mlstm):
    it = iter(refs)
    take = lambda n: [next(it) for _ in range(n)]
    r_in, m_in = take(N_RWKV_IN), take(N_MLSTM_IN)
    r_out, m_out = take(N_RWKV_OUT), take(N_MLSTM_OUT)
    r_scr = take(N_RWKV_SCRATCH)
    m_scr = list(it)
    work = [_rwkv_steps(*r_in, *r_out, *r_scr, *_chunk_pos(), **rwkv),
            _mlstm_steps(*m_in, *m_out, *m_scr, *_chunk_pos(), **mlstm)]
    while work:
        for gen in list(work):
            try:
                next(gen)
            except StopIteration:
                work.remove(gen)


def _state_spec(G, shape, l):
    zeros = (0,) * len(shape)
    return pl.BlockSpec((None, G) + tuple(shape), lambda b, c: (l, b) + zeros)


def _param_spec(shape, l):
    zeros = (0,) * len(shape)
    return pl.BlockSpec((None,) + tuple(shape), lambda b, c: (l,) + zeros)


def _mixer_params():
    return pltpu.CompilerParams(dimension_semantics=("parallel", "arbitrary"), vmem_limit_bytes=VMEM_LIMIT)


class _SeqLayout:
    def __init__(self, B, T, chunk, step_rows, max_seqs):
        self.B, self.T = B, T
        self.packed = T < SHORT_SEQ_ROWS
        if self.packed:
            self.tb, self.L, self.nc = T, SHORT_SEQ_ROWS, 1
        else:
            self.tb = self.L = min(T, chunk)
            self.nc = T // self.L
            assert T % self.L == 0
        assert self.L & (self.L - 1) == 0
        self.G = max(1, min(B, step_rows // self.L, max_seqs))
        assert B % self.G == 0
        self.grid = (B // self.G, self.nc)
        self.static = dict(G=self.G, L=self.L, lv=self.tb, packed=self.packed)

    def view(self, a):
        shape = (self.B // self.G, self.G * self.T) if self.packed else (self.B, self.T)
        return a.reshape(shape + (a.shape[-1],))

    def spec(self, width):
        if self.packed:
            return pl.BlockSpec((1, self.G * self.T, width), lambda b, c: (b, 0, 0))
        return pl.BlockSpec((self.G, self.tb, width), lambda b, c: (b, c, 0))

    def out_shape(self, width):
        shape = (self.B // self.G, self.G * self.T) if self.packed else (self.B, self.T)
        return jax.ShapeDtypeStruct(shape + (width,), F32)


def _rwkv_call(pr, st, W, l, lay):
    G, L = lay.G, lay.L
    vec = lambda n: _param_spec((1, n), l)
    lora = _param_spec((LANES, D_RWKV), l)
    states = [_state_spec(G, (1, P_RWKV), l), _state_spec(G, (H_R, HD_R, HD_R), l)]
    names = ("shift", "wkv")
    in_specs = [lay.spec(P_RWKV)] + states + [vec(P_RWKV), vec(D_RWKV), lora, vec(D_RWKV), lora, lora] \
        + [vec(D_RWKV)] * 5
    out_specs = [lay.spec(D_RWKV)] + states
    out_shape = [lay.out_shape(D_RWKV)] + [jax.ShapeDtypeStruct(st[n].shape, F32) for n in names]
    scratch = [pltpu.VMEM((G, 8 + L, P_RWKV), F32), pltpu.VMEM((G, N_PAIR, LANES, LANES), F32)]
    operands = [lay.view(pr), st["shift"], st["wkv"], W["rwkv_mu"], W["rwkv_w0"], W["rwkv_w_up"], W["rwkv_a0"],
                W["rwkv_a_up"], W["rwkv_g_up"], W["rwkv_k_k"], W["rwkv_k_a"], W["rwkv_r_k"], W["rwkv_ln_w"],
                W["rwkv_ln_b"]]
    assert len(in_specs) == N_RWKV_IN == len(operands)
    return in_specs, out_specs, out_shape, scratch, operands, names


def _mlstm_call(pm, pc, st, W, l, lay):
    G, L = lay.G, lay.L
    vec = lambda n: _param_spec((1, n), l)
    sq = _param_spec((D_MPAD, D_MPAD), l)
    gate = _param_spec((D_MPAD, LANES), l)
    states = [_state_spec(G, (MCONV_W - 1, D_MLSTM), l), _state_spec(G, (H_M, HD_M, HD_M), l),
              _state_spec(G, (H_M, HD_M), l), _state_spec(G, (1, H_M), l),
              _state_spec(G, (CONV_W - 1, D_CONV), l)]
    names = ("mconv", "mC", "mn", "mm", "conv")
    in_specs = [lay.spec(2 * D_MPAD), lay.spec(3 * D_CONV)] + states \
        + [_param_spec((MCONV_W, D_MPAD), l), vec(D_MPAD), sq, sq, sq, gate, gate, gate, vec(LANES),
           vec(D_MPAD), vec(D_MPAD), _param_spec((CONV_W, D_CONV), l)]
    out_specs = [lay.spec(D_MPAD), lay.spec(D_CONV)] + states
    out_shape = [lay.out_shape(D_MPAD), lay.out_shape(D_CONV)] + [jax.ShapeDtypeStruct(st[n].shape, F32) for n in names]
    scratch = [pltpu.VMEM((G, 8 + L, D_MPAD), F32), pltpu.VMEM((G, H_M, LANES, LANES), F32),
               pltpu.VMEM((G, H_M, LANES), F32), pltpu.VMEM((G, 8 + L, D_CONV), F32)]
    operands = [lay.view(pm), lay.view(pc)] + [st[n] for n in names] + [
        W["mlstm_conv_w"], W["mlstm_conv_b"], W["mlstm_wq"], W["mlstm_wk"], W["mlstm_wv"], W["mlstm_wgq"],
        W["mlstm_wgk"], W["mlstm_wgv"], W["mlstm_b_gate"], W["mlstm_gn_w"], W["mlstm_skip"], W["conv_w"]]
    assert len(in_specs) == N_MLSTM_IN == len(operands)
    return in_specs, out_specs, out_shape, scratch, operands, names


def _layer_mixers(p, st, W, l, B, T):
    pc, pr, pm = p
    n_tok = B * T
    fused = T < SHORT_SEQ_ROWS
    rk = _SeqLayout(B, T, RWKV_CHUNK, RWKV_STEP_ROWS, FUSED_MAX_SEQS if fused else RWKV_MAX_SEQS)
    ml = _SeqLayout(B, T, MLSTM_CHUNK, MLSTM_STEP_ROWS, FUSED_MAX_SEQS if fused else MLSTM_MAX_SEQS)
    r_in, r_out, r_shape, r_scr, r_ops, r_names = _rwkv_call(pr, st, W, l, rk)
    m_in, m_out, m_shape, m_scr, m_ops, m_names = _mlstm_call(pm, pc, st, W, l, ml)
    r_alias = {1: 1, 2: 2}
    m_alias = {2 + k: 2 + k for k in range(len(m_names))}
    if fused:
        assert rk.grid == ml.grid
        aliases = dict(r_alias)
        aliases.update({N_RWKV_IN + i: N_RWKV_OUT + o for i, o in m_alias.items()})
        outs = pl.pallas_call(
            functools.partial(_mixers_body, rwkv=rk.static, mlstm=ml.static), grid=rk.grid,
            in_specs=r_in + m_in, out_specs=r_out + m_out, out_shape=r_shape + m_shape,
            input_output_aliases=aliases, scratch_shapes=r_scr + m_scr,
            compiler_params=_mixer_params(), name="mixers")(*r_ops, *m_ops)
        r_res, m_res = outs[:N_RWKV_OUT], outs[N_RWKV_OUT:]
    else:
        r_res = pl.pallas_call(
            functools.partial(_rwkv_body, **rk.static), grid=rk.grid, in_specs=r_in, out_specs=r_out,
            out_shape=r_shape, input_output_aliases=r_alias, scratch_shapes=r_scr,
            compiler_params=_mixer_params(), name="rwkv7")(*r_ops)
        m_res = pl.pallas_call(
            functools.partial(_mlstm_body, **ml.static), grid=ml.grid, in_specs=m_in, out_specs=m_out,
            out_shape=m_shape, input_output_aliases=m_alias, scratch_shapes=m_scr,
            compiler_params=_mixer_params(), name="mlstm_conv")(*m_ops)
    for n, o in zip(r_names, r_res[1:]):
        st[n] = o
    for n, o in zip(m_names, m_res[2:]):
        st[n] = o
    return m_res[1].reshape(n_tok, D_CONV), r_res[0].reshape(n_tok, D_RWKV), m_res[0].reshape(n_tok, D_MPAD)


def _pad_heads(x, axis):
    axis = axis % x.ndim
    shp = x.shape
    x = x.reshape(shp[:axis] + (H_M, HD_M) + shp[axis + 1:])
    pad = [(0, 0)] * x.ndim
    pad[axis + 1] = (0, LANES - HD_M)
    x = jnp.pad(x, pad)
    return x.reshape(shp[:axis] + (D_MPAD,) + shp[axis + 1:])


def _head_block_diag(w):
    wp = jnp.pad(w, ((0, 0), (0, 0), (0, LANES - HD_M), (0, LANES - HD_M)))
    eye = jnp.eye(H_M, dtype=w.dtype)
    return jnp.einsum("lhde,hg->lhdge", wp, eye).reshape(w.shape[0], D_MPAD, D_MPAD)


def _prep_weights(norm_ff, ffn_gate, ffn_up, ffn_down, norm_mix, w_in, w_out, conv_w,
                  rwkv_mu, rwkv_w0, rwkv_w_up, rwkv_a0, rwkv_a_up, rwkv_g_up, rwkv_k_k, rwkv_k_a, rwkv_r_k,
                  rwkv_ln_w, rwkv_ln_b, mlstm_conv_w, mlstm_conv_b, mlstm_wq, mlstm_wk, mlstm_wv,
                  mlstm_w_gate, mlstm_b_gate, mlstm_gn_w, mlstm_skip, norm_final):
    depth = w_in.shape[0]
    pc, pr = 3 * D_CONV, P_RWKV
    row = lambda x: x[:, None, :]
    W = {}
    W["norm_ff"] = norm_ff[:, :, None, :]
    W["ffn_gate"] = ffn_gate.astype(BF16)
    W["ffn_up"] = ffn_up.astype(BF16)
    W["ffn_down"] = ffn_down.astype(BF16)
    W["norm_mix"] = row(norm_mix)
    W["w_in_c"] = w_in[:, :, :pc].astype(BF16)
    W["w_in_r"] = w_in[:, :, pc:pc + pr].astype(BF16)
    w_u = _pad_heads(w_in[:, :, pc + pr:pc + pr + D_MLSTM], 2)
    w_o = _pad_heads(w_in[:, :, pc + pr + D_MLSTM:], 2)
    W["w_in_m"] = jnp.concatenate([w_u, w_o], axis=2).astype(BF16)
    W["w_out_c"] = w_out[:, :D_CONV].astype(BF16)
    W["w_out_r"] = w_out[:, D_CONV:D_CONV + D_RWKV].astype(BF16)
    W["w_out_m"] = _pad_heads(w_out[:, D_CONV + D_RWKV:], 1).astype(BF16)
    W["conv_w"] = conv_w
    W["rwkv_mu"] = row(rwkv_mu)
    W["rwkv_w0"] = row(rwkv_w0)
    W["rwkv_a0"] = row(rwkv_a0)
    zl = lambda n: jnp.zeros((depth, n, D_RWKV), F32)
    W["rwkv_w_up"] = jnp.concatenate([rwkv_w_up, zl(LANES - LORA_W)], axis=1)
    W["rwkv_a_up"] = jnp.concatenate([zl(LORA_W), rwkv_a_up, zl(LORA_G)], axis=1)
    W["rwkv_g_up"] = jnp.concatenate([zl(LORA_W + LORA_A), rwkv_g_up], axis=1)
    W["rwkv_k_k"] = row(rwkv_k_k)
    W["rwkv_k_a"] = row(rwkv_k_a)
    W["rwkv_r_k"] = row(rwkv_r_k.reshape(depth, D_RWKV))
    W["rwkv_ln_w"] = row(rwkv_ln_w)
    W["rwkv_ln_b"] = row(rwkv_ln_b)
    W["mlstm_conv_w"] = _pad_heads(mlstm_conv_w, 2)
    W["mlstm_conv_b"] = row(_pad_heads(mlstm_conv_b, 1))
    W["mlstm_wq"] = _head_block_diag(mlstm_wq)
    W["mlstm_wk"] = _head_block_diag(mlstm_wk)
    W["mlstm_wv"] = _head_block_diag(mlstm_wv)
    wg = mlstm_w_gate.reshape(depth, H_M, 3, HD_M, 2 * H_M)
    for i, name in enumerate(("mlstm_wgq", "mlstm_wgk", "mlstm_wgv")):
        part = jnp.pad(wg[:, :, i], ((0, 0), (0, 0), (0, LANES - HD_M), (0, LANES - 2 * H_M)))
        W[name] = part.reshape(depth, D_MPAD, LANES)
    W["mlstm_b_gate"] = row(jnp.pad(mlstm_b_gate, ((0, 0), (0, LANES - 2 * H_M))))
    W["mlstm_gn_w"] = row(_pad_heads(mlstm_gn_w, 1))
    W["mlstm_skip"] = row(_pad_heads(mlstm_skip, 1))
    W["norm_final"] = norm_final[None, :]
    return W


_STATE_NAMES = ("conv", "shift", "wkv", "mconv", "mC", "mn", "mm")


def _run_trunks(x_p, states_p, x_s, states_s, W):
    (Bp, Tp, D), (Bs, Ts, _) = x_p.shape, x_s.shape
    depth = W["w_in_c"].shape[0]
    st_p, st_s = dict(zip(_STATE_NAMES, states_p)), dict(zip(_STATE_NAMES, states_s))
    for st in (st_p, st_s):
        st["mm"] = st["mm"][:, :, None, :]
    xp, xs = x_p.reshape(Bp * Tp, D), x_s.reshape(Bs * Ts, D)
    for l in range(depth):
        xp, xs = _ffn(xp, xs, W, l, 0)
        p_p, p_s = _inproj(xp, xs, W, l)
        y_p = _layer_mixers(p_p, st_p, W, l, Bp, Tp)
        y_s = _layer_mixers(p_s, st_s, W, l, Bs, Ts)
        xp, xs = _mix_ffn(xp, y_p, xs, y_s, W, l, final=(l == depth - 1))
    for st in (st_p, st_s):
        st["mm"] = st["mm"][:, :, 0, :]
    return ((xp.reshape(Bp, Tp, D), xs.reshape(Bs, Ts, D))
            + tuple(st_p[n] for n in _STATE_NAMES) + tuple(st_s[n] for n in _STATE_NAMES))


def kernel(x_prompt, x_sample, cache_conv, cache_shift, state_wkv, cache_mconv, state_mC, state_mn, state_mm, norm_ff, ffn_gate, ffn_up, ffn_down, norm_mix, w_in, w_out, conv_w, rwkv_mu, rwkv_w0, rwkv_w_up, rwkv_a0, rwkv_a_up, rwkv_g_up, rwkv_k_k, rwkv_k_a, rwkv_r_k, rwkv_ln_w, rwkv_ln_b, mlstm_conv_w, mlstm_conv_b, mlstm_wq, mlstm_wk, mlstm_wv, mlstm_w_gate, mlstm_b_gate, mlstm_gn_w, mlstm_skip, norm_final):
    W = _prep_weights(norm_ff, ffn_gate, ffn_up, ffn_down, norm_mix, w_in, w_out, conv_w,
                      rwkv_mu, rwkv_w0, rwkv_w_up, rwkv_a0, rwkv_a_up, rwkv_g_up, rwkv_k_k, rwkv_k_a, rwkv_r_k,
                      rwkv_ln_w, rwkv_ln_b, mlstm_conv_w, mlstm_conv_b, mlstm_wq, mlstm_wk, mlstm_wv,
                      mlstm_w_gate, mlstm_b_gate, mlstm_gn_w, mlstm_skip, norm_final)
    depth = w_in.shape[0]
    Bp = x_prompt.shape[0]
    zeros = lambda *shape: jnp.zeros((depth, Bp) + shape, F32)
    st_p = (zeros(CONV_W - 1, D_CONV), zeros(1, P_RWKV), zeros(H_R, HD_R, HD_R), zeros(MCONV_W - 1, D_MLSTM),
            zeros(H_M, HD_M, HD_M), zeros(H_M, HD_M), zeros(H_M))
    st_s = (cache_conv, cache_shift, state_wkv, cache_mconv, state_mC, state_mn, state_mm)
    return _run_trunks(x_prompt, st_p, x_sample, st_s, W)
```
